```python
import math
import jax, jax.numpy as jnp
from jax import lax
import numpy as np

D_MODEL = 1024
BATCH = 32
SEQ = 2048
DEPTH = 2

HEAD_DIM = 64
D_MIX = D_MODEL
SSM_GROUP = 16
SSM_WIDTH = D_MIX // 4
SSM_GROUPS = SSM_WIDTH // SSM_GROUP
SSM_STATE = 64
DSA_HEADS = (D_MIX - SSM_WIDTH) // (2 * HEAD_DIM)
FOX_HEADS = DSA_HEADS
DSA_WIDTH = DSA_HEADS * HEAD_DIM
FOX_WIDTH = FOX_HEADS * HEAD_DIM
DSA_PATTERNS = ((128, 1), (512, 4), (2048, 16))
DSA_BLOCK = 128
FOX_BLOCK = 128
D_FF = ((8 * D_MODEL // 3 + 255) // 256) * 256
IN_SIZES = (SSM_WIDTH, 3 * DSA_WIDTH, 3 * FOX_WIDTH, FOX_HEADS)
N_IN = sum(IN_SIZES)
RMS_EPS = 1e-6

kernel_name = 'hybrid_s5_dilated_fox_block'


def rms_norm(x, g):
    xf = x.astype(jnp.float32)
    y = xf * lax.rsqrt(jnp.mean(xf * xf, axis=-1, keepdims=True) + RMS_EPS)
    return (y * g.astype(jnp.float32)).astype(x.dtype)


def _complex_affine_combine(e1, e2):
    a1r, a1i, b1r, b1i = e1
    a2r, a2i, b2r, b2i = e2
    ar = a2r * a1r - a2i * a1i
    ai = a2r * a1i + a2i * a1r
    br = a2r * b1r - a2i * b1i + b2r
    bi = a2r * b1i + a2i * b1r + b2i
    return (ar, ai, br, bi)


def s5_mixer(u, log_dt, a_re, a_im, b_re, b_im, c_re, c_im, d_skip, w_glu, b_glu):
    bsz, seq, _ = u.shape
    dtype = u.dtype
    f32 = jnp.float32
    dt = jnp.exp(log_dt.astype(f32))[:, None]
    ar, ai = a_re.astype(f32), a_im.astype(f32)
    decay = jnp.exp(ar * dt)
    abar_re = decay * jnp.cos(ai * dt)
    abar_im = decay * jnp.sin(ai * dt)
    mag2 = ar * ar + ai * ai
    coef_re = ((abar_re - 1.0) * ar + abar_im * ai) / mag2
    coef_im = (abar_im * ar - (abar_re - 1.0) * ai) / mag2
    br, bi = b_re.astype(f32), b_im.astype(f32)
    bbar_re = coef_re[..., None] * br - coef_im[..., None] * bi
    bbar_im = coef_re[..., None] * bi + coef_im[..., None] * br
    ug = u.reshape(bsz, seq, SSM_GROUPS, SSM_GROUP)
    x_re = jnp.einsum('bsgc,gpc->sbgp', ug, bbar_re.astype(dtype))
    x_im = jnp.einsum('bsgc,gpc->sbgp', ug, bbar_im.astype(dtype))
    shape = x_re.shape
    elems = (jnp.broadcast_to(abar_re.astype(dtype), shape),
             jnp.broadcast_to(abar_im.astype(dtype), shape), x_re, x_im)
    _, _, h_re, h_im = lax.associative_scan(_complex_affine_combine, elems, axis=0)
    y = (jnp.einsum('sbgp,gcp->bsgc', h_re, c_re)
         - jnp.einsum('sbgp,gcp->bsgc', h_im, c_im))
    y = y.reshape(bsz, seq, SSM_WIDTH) + d_skip * u
    g = jax.nn.gelu(y)
    return g * jax.nn.sigmoid(g @ w_glu + b_glu)


def dilated_window_branch(q, k, v, window, dilation):
    bsz, seq, nh, dh = q.shape
    span = window // dilation
    blk = DSA_BLOCK
    length = seq // dilation
    nb = -(-length // blk)
    lp = nb * blk

    def to_strided(t):
        t = t.reshape(bsz, length, dilation, nh, dh)
        return jnp.pad(t, ((0, 0), (0, lp - length), (0, 0), (0, 0), (0, 0)))

    def key_blocks(t):
        t = jnp.pad(to_strided(t), ((0, 0), (blk, 0), (0, 0), (0, 0), (0, 0)))
        t = t.reshape(bsz, nb + 1, blk, dilation, nh, dh)
        return jnp.concatenate([t[:, :-1], t[:, 1:]], axis=2)

    qs = to_strided(q).reshape(bsz, nb, blk, dilation, nh, dh)
    ks, vs = key_blocks(k), key_blocks(v)
    s = jnp.einsum('bnqrhd,bnkrhd->bnrhqk', qs, ks).astype(jnp.float32)
    qi = jnp.arange(blk)[:, None]
    ki = jnp.arange(2 * blk)[None, :]
    dist = blk + qi - ki
    key_row = (jnp.arange(nb) * blk - blk)[:, None, None] + ki[None]
    valid = (dist >= 0) & (dist <= span) & (key_row >= 0)
    s = jnp.where(valid[None, :, None, None], s, -jnp.inf)
    m = jnp.max(s, axis=-1, keepdims=True)
    p = jnp.exp(s - m)
    den = jnp.sum(p, axis=-1, keepdims=True)
    o = jnp.einsum('bnrhqk,bnkrhd->bnqrhd', (p / den).astype(v.dtype), vs)
    lse = (m + jnp.log(den))[..., 0]
    o = o.reshape(bsz, lp, dilation, nh, dh)[:, :length].reshape(bsz, seq, nh, dh)
    lse = lse.transpose(0, 1, 4, 2, 3).reshape(bsz, lp, dilation, nh)[:, :length]
    return o, lse.reshape(bsz, seq, nh)


def dilated_attention(q, k, v):
    results = [dilated_window_branch(q, k, v, w, d) for (w, d) in DSA_PATTERNS]
    outs = jnp.stack([r[0] for r in results], axis=0)
    lses = jnp.stack([r[1] for r in results], axis=0)
    wts = jax.nn.softmax(lses, axis=0).astype(q.dtype)
    return jnp.einsum('pbsh,pbshd->bshd', wts, outs)


def forgetting_attention(q, k, v, log_f):
    bsz, seq, nh, dh = q.shape
    cum = jnp.cumsum(log_f, axis=1).transpose(0, 2, 1)
    outs = []
    for i in range(seq // FOX_BLOCK):
        q0, q1 = i * FOX_BLOCK, (i + 1) * FOX_BLOCK
        s = jnp.einsum('bqhd,bkhd->bhqk', q[:, q0:q1], k[:, :q1]).astype(jnp.float32)
        bias = cum[:, :, q0:q1, None] - cum[:, :, None, :q1]
        causal = jnp.arange(q0, q1)[:, None] >= jnp.arange(q1)[None, :]
        p = jax.nn.softmax(jnp.where(causal, s + bias, -jnp.inf), axis=-1)
        outs.append(jnp.einsum('bhqk,bkhd->bqhd', p.astype(v.dtype), v[:, :q1]))
    return jnp.concatenate(outs, axis=1)


def setup_inputs(seed: int = 0) -> dict:
    key = jax.random.key(seed)
    ks = jax.random.split(key, 24)
    f32 = jnp.float32
    L, D, G, P, C = DEPTH, D_MODEL, SSM_GROUPS, SSM_STATE, SSM_GROUP

    def nrm(k, shape, scale):
        return jax.random.normal(k, shape, f32) * scale

    def gain(k, shape):
        return 1.0 + 0.02 * jax.random.normal(k, shape, f32)

    n_idx = jnp.arange(P, dtype=f32)
    return {
        'x': jax.random.normal(ks[0], (BATCH, SEQ, D), f32),
        'g_pre_mix': gain(ks[1], (L, D)),
        'w_in': nrm(ks[2], (L, D, N_IN), D ** -0.5),
        'b_f': 1.0 + 3.0 * jax.random.uniform(ks[3], (L, FOX_HEADS), f32),
        'ssm_log_dt': jax.random.uniform(ks[4], (L, G), f32, math.log(1e-3), math.log(1e-1)),
        'ssm_a_re': -0.5 + 0.01 * jax.random.normal(ks[5], (L, G, P), f32),
        'ssm_a_im': math.pi * n_idx + 0.01 * jax.random.normal(ks[6], (L, G, P), f32),
        'ssm_b_re': nrm(ks[7], (L, G, P, C), (2 * C) ** -0.5),
        'ssm_b_im': nrm(ks[8], (L, G, P, C), (2 * C) ** -0.5),
        'ssm_c_re': nrm(ks[9], (L, G, C, P), (2 * P) ** -0.5),
        'ssm_c_im': nrm(ks[10], (L, G, C, P), (2 * P) ** -0.5),
        'ssm_d': nrm(ks[11], (L, SSM_WIDTH), 1.0),
        'w_glu': nrm(ks[12], (L, SSM_WIDTH, SSM_WIDTH), SSM_WIDTH ** -0.5),
        'b_glu': nrm(ks[13], (L, SSM_WIDTH), 0.01),
        'g_mix': gain(ks[14], (L, D_MIX)),
        'w_out': nrm(ks[15], (L, D_MIX, D), D_MIX ** -0.5),
        'g_post_mix': gain(ks[16], (L, D)),
        'g_pre_ffn': gain(ks[17], (L, D)),
        'w_gate_up': nrm(ks[18], (L, D, 2 * D_FF), D ** -0.5),
        'w_down': nrm(ks[19], (L, D_FF, D), D_FF ** -0.5),
        'g_post_ffn': gain(ks[20], (L, D)),
    }


def reference(x, g_pre_mix, w_in, b_f, ssm_log_dt, ssm_a_re, ssm_a_im, ssm_b_re, ssm_b_im,
              ssm_c_re, ssm_c_im, ssm_d, w_glu, b_glu, g_mix, w_out, g_post_mix,
              g_pre_ffn, w_gate_up, w_down, g_post_ffn):
    bsz, seq, _ = x.shape
    scale = HEAD_DIM ** -0.5
    offsets = [int(o) for o in np.cumsum(IN_SIZES)[:-1]]
    for l in range(DEPTH):
        h = rms_norm(x, g_pre_mix[l])
        proj = h @ w_in[l]
        u_ssm, qkv_dsa, qkv_fox, f_logit = jnp.split(proj, offsets, axis=-1)

        y_ssm = s5_mixer(u_ssm, ssm_log_dt[l], ssm_a_re[l], ssm_a_im[l], ssm_b_re[l],
                         ssm_b_im[l], ssm_c_re[l], ssm_c_im[l], ssm_d[l], w_glu[l], b_glu[l])

        qkv_dsa = qkv_dsa.reshape(bsz, seq, 3, DSA_HEADS, HEAD_DIM)
        y_dsa = dilated_attention(qkv_dsa[:, :, 0] * scale, qkv_dsa[:, :, 1], qkv_dsa[:, :, 2])

        qkv_fox = qkv_fox.reshape(bsz, seq, 3, FOX_HEADS, HEAD_DIM)
        log_f = jax.nn.log_sigmoid((f_logit + b_f[l]).astype(jnp.float32))
        y_fox = forgetting_attention(qkv_fox[:, :, 0] * scale, qkv_fox[:, :, 1],
                                     qkv_fox[:, :, 2], log_f)

        gm = g_mix[l]
        mixed = jnp.concatenate([
            rms_norm(y_ssm, gm[:SSM_WIDTH]),
            rms_norm(y_dsa.reshape(bsz, seq, DSA_WIDTH), gm[SSM_WIDTH:SSM_WIDTH + DSA_WIDTH]),
            rms_norm(y_fox.reshape(bsz, seq, FOX_WIDTH), gm[SSM_WIDTH + DSA_WIDTH:]),
        ], axis=-1)
        x = x + rms_norm(mixed @ w_out[l], g_post_mix[l])

        h = rms_norm(x, g_pre_ffn[l])
        gate, up = jnp.split(h @ w_gate_up[l], 2, axis=-1)
        x = x + rms_norm((jax.nn.silu(gate) * up) @ w_down[l], g_post_ffn[l])
    return x
```

```python
import functools
import math

import jax
import jax.numpy as jnp
from jax import lax
from jax.experimental import pallas as pl
from jax.experimental.pallas import tpu as pltpu

F32 = jnp.float32
BF16 = jnp.bfloat16

HEAD_DIM = 64
SSM_GROUP = 16
SSM_STATE = 64
SSM_WIDTH = 256
ATT_WIDTH = 384
N_HEADS = 6
HEAD_PAIRS = N_HEADS // 2
DSA_SPAN = 128
DSA_BLOCK = 128
DSA_DILATIONS = (1, 4, 16)
RMS_EPS = 1e-6
NEG = -1e30
QK_SCALE = HEAD_DIM ** -0.5

LANES = 128
VMEM_LIMIT = 56 * 1024 * 1024

NT_DIMS = (((1,), (1,)), ((), ()))


def _rms_scale(x, g):
    ms = jnp.mean(x * x, axis=-1, keepdims=True)
    return x * lax.rsqrt(ms + RMS_EPS) * g


def _params(*sem):
    return pltpu.CompilerParams(dimension_semantics=sem, vmem_limit_bytes=VMEM_LIMIT)


def _premix_kernel(x_ref, g_ref, wu_ref, wd_ref, wf_ref, wft_ref, u_ref, qd_ref, qf_ref, ft_ref):
    h = _rms_scale(x_ref[0], g_ref[...]).astype(BF16)
    u_ref[...] = jnp.dot(h, wu_ref[...], preferred_element_type=F32)
    qd_ref[0] = jnp.dot(h, wd_ref[...], preferred_element_type=F32)
    qf_ref[0] = jnp.dot(h, wf_ref[...], preferred_element_type=F32).astype(BF16)
    ft_ref[0] = lax.dot_general(wft_ref[...], h, NT_DIMS, preferred_element_type=F32)


def _premix(x, g, wu, wd, wf, wft, *, tm):
    B, S, D = x.shape
    grid = (B, S // tm)
    const = lambda b, j: (0, 0)
    return pl.pallas_call(
        _premix_kernel,
        grid=grid,
        in_specs=[
            pl.BlockSpec((1, tm, D), lambda b, j: (b, j, 0)),
            pl.BlockSpec((1, D), const),
            pl.BlockSpec(wu.shape, const),
            pl.BlockSpec(wd.shape, const),
            pl.BlockSpec(wf.shape, const),
            pl.BlockSpec(wft.shape, const),
        ],
        out_specs=[
            pl.BlockSpec((tm, SSM_WIDTH), lambda b, j: (j, b)),
            pl.BlockSpec((1, tm, 3 * ATT_WIDTH), lambda b, j: (b, j, 0)),
            pl.BlockSpec((1, tm, 3 * ATT_WIDTH), lambda b, j: (b, j, 0)),
            pl.BlockSpec((1, 8, tm), lambda b, j: (b, 0, j)),
        ],
        out_shape=[
            jax.ShapeDtypeStruct((S, B * SSM_WIDTH), F32),
            jax.ShapeDtypeStruct((B, S, 3 * ATT_WIDTH), F32),
            jax.ShapeDtypeStruct((B, S, 3 * ATT_WIDTH), BF16),
            jax.ShapeDtypeStruct((B, 8, S), F32),
        ],
        compiler_params=_params("parallel", "parallel"),
        name="premix",
    )(x, g, wu, wd, wf, wft)


def _s5_kernel(u_ref, bt_ref, are_ref, aim_ref, ct_ref, d_ref, wg_ref, bg_ref, y_ref,
               xs_ref, h_ref, *, batch, steps, col_tile):
    n_state = are_ref.shape[1]

    @pl.when(pl.program_id(0) == 0)
    def _():
        h_ref[...] = jnp.zeros_like(h_ref)

    u = u_ref[...]
    xs_ref[...] = jnp.dot(u.astype(BF16), bt_ref[...], preferred_element_type=F32)

    for c in range(n_state // col_tile):
        re = slice(c * col_tile, (c + 1) * col_tile)
        im = slice(n_state + c * col_tile, n_state + (c + 1) * col_tile)
        ar = are_ref[:, re]
        ai = aim_ref[:, re]

        def step(s, carry, re=re, im=im, ar=ar, ai=ai):
            hr, hi = carry
            rows = pl.ds(pl.multiple_of(s * batch, batch), batch)
            nr = ar * hr - ai * hi + xs_ref[rows, re]
            ni = ar * hi + ai * hr + xs_ref[rows, im]
            xs_ref[rows, re] = nr
            xs_ref[rows, im] = ni
            return nr, ni

        hr, hi = lax.fori_loop(0, steps, step, (h_ref[:, re], h_ref[:, im]), unroll=2)
        h_ref[:, re] = hr
        h_ref[:, im] = hi

    y = jnp.dot(xs_ref[...].astype(BF16), ct_ref[...], preferred_element_type=F32)
    y = y + d_ref[...] * u
    g = jax.nn.gelu(y)
    z = jnp.dot(g.astype(BF16), wg_ref[...], preferred_element_type=F32) + bg_ref[...]
    y_ref[...] = g * jax.nn.sigmoid(z)


def _s5(u_sb, bt, a_re, a_im, ct, d, wg, bg, *, batch, steps):
    rows_total, width = u_sb.shape
    rows = steps * batch
    n_cols = bt.shape[1]
    const = lambda i: (0, 0)
    kern = functools.partial(_s5_kernel, batch=batch, steps=steps, col_tile=2 * LANES)
    return pl.pallas_call(
        kern,
        grid=(rows_total // rows,),
        in_specs=[
            pl.BlockSpec((rows, width), lambda i: (i, 0)),
            pl.BlockSpec(bt.shape, const),
            pl.BlockSpec(a_re.shape, const),
            pl.BlockSpec(a_im.shape, const),
            pl.BlockSpec(ct.shape, const),
            pl.BlockSpec(d.shape, const),
            pl.BlockSpec(wg.shape, const),
            pl.BlockSpec(bg.shape, const),
        ],
        out_specs=pl.BlockSpec((rows, width), lambda i: (i, 0)),
        out_shape=jax.ShapeDtypeStruct((rows_total, width), F32),
        scratch_shapes=[
            pltpu.VMEM((rows, n_cols), F32),
            pltpu.VMEM((batch, n_cols), F32),
        ],
        compiler_params=_params("arbitrary"),
        name="s5",
    )(u_sb, bt, a_re, a_im, ct, d, wg, bg)


def _dsa_attend(qb, kw, vw, valid, head0):
    parts = []
    for h in range(2):
        qh = jnp.where(head0 if h == 0 else jnp.logical_not(head0), qb, jnp.zeros_like(qb))
        s = lax.dot_general(qh, kw, NT_DIMS, preferred_element_type=F32)
        s = jnp.where(valid, s, NEG)
        m = jnp.max(s, axis=-1, keepdims=True)
        p = jnp.exp(s - m)
        l = jnp.sum(p, axis=-1, keepdims=True)
        n = jnp.dot(p.astype(BF16), vw, preferred_element_type=F32)
        parts.append((n, m, l))
    (n0, m0, l0), (n1, m1, l1) = parts
    shape = n0.shape
    num = jnp.where(head0, n0, n1)
    mx = jnp.where(head0, jnp.broadcast_to(m0, shape), jnp.broadcast_to(m1, shape))
    den = jnp.where(head0, jnp.broadcast_to(l0, shape), jnp.broadcast_to(l1, shape))
    return num, mx, den


def _softmax_merge(n1, m1, l1, n2, m2, l2):
    m = jnp.maximum(m1, m2)
    e1 = jnp.exp(m1 - m)
    e2 = jnp.exp(m2 - m)
    return n1 * e1 + n2 * e2, m, l1 * e1 + l2 * e2


def _dsa_kernel(q_ref, k_ref, v_ref, o_ref, acc_ref, m_ref, l_ref, *, seq):
    blk = DSA_BLOCK
    n_blocks = seq // blk
    head0 = lax.broadcasted_iota(jnp.int32, (blk, LANES), 1) < HEAD_DIM
    qi = lax.broadcasted_iota(jnp.int32, (blk, 2 * blk), 0)
    ki = lax.broadcasted_iota(jnp.int32, (blk, 2 * blk), 1)
    band = (ki >= qi) & (ki <= qi + DSA_SPAN)
    cur_half = ki >= blk
    causal = (lax.broadcasted_iota(jnp.int32, (blk, blk), 1)
              <= lax.broadcasted_iota(jnp.int32, (blk, blk), 0))

    def rows(start, d):
        return pl.ds(start, blk) if d == 1 else pl.ds(start, blk, stride=d)

    def load_q(start, d):
        return (q_ref[0, rows(start, d), :] * QK_SCALE).astype(BF16)

    def banded(i, d):
        per_res = n_blocks // d
        r = i // per_res
        n = i % per_res
        start = r + d * blk * n
        prev = r + d * blk * jnp.maximum(n - 1, 0)
        kw = jnp.concatenate([k_ref[0, rows(prev, d), :], k_ref[0, rows(start, d), :]], axis=0)
        vw = jnp.concatenate([v_ref[0, rows(prev, d), :], v_ref[0, rows(start, d), :]], axis=0)
        valid = band & (cur_half | (n > 0))
        out = _dsa_attend(load_q(start, d), kw.astype(BF16), vw.astype(BF16), valid, head0)
        return start, out

    def pattern_dense(i, carry):
        start, (num, mx, den) = banded(i, 1)
        acc_ref[rows(start, 1), :] = num
        m_ref[rows(start, 1), :] = mx
        l_ref[rows(start, 1), :] = den
        return carry

    def pattern_mid(i, carry):
        d = DSA_DILATIONS[1]
        start, (num, mx, den) = banded(i, d)
        rs = rows(start, d)
        num, mx, den = _softmax_merge(acc_ref[rs, :], m_ref[rs, :], l_ref[rs, :], num, mx, den)
        acc_ref[rs, :] = num
        m_ref[rs, :] = mx
        l_ref[rs, :] = den
        return carry

    def pattern_wide(i, carry):
        d = DSA_DILATIONS[2]
        rs = rows(i, d)
        kw = k_ref[0, rs, :].astype(BF16)
        vw = v_ref[0, rs, :].astype(BF16)
        num, mx, den = _dsa_attend(load_q(i, d), kw, vw, causal, head0)
        num, mx, den = _softmax_merge(acc_ref[rs, :], m_ref[rs, :], l_ref[rs, :], num, mx, den)
        o_ref[0, rs, :] = num / den
        return carry

    lax.fori_loop(0, n_blocks, pattern_dense, 0)
    lax.fori_loop(0, n_blocks, pattern_mid, 0)
    lax.fori_loop(0, n_blocks, pattern_wide, 0)


def _dsa(qkv):
    B, S, _ = qkv.shape
    assert S == DSA_DILATIONS[2] * DSA_BLOCK, S
    blockspec = lambda off: pl.BlockSpec((1, S, LANES), lambda b, p: (b, 0, off + p))
    return pl.pallas_call(
        functools.partial(_dsa_kernel, seq=S),
        grid=(B, HEAD_PAIRS),
        in_specs=[blockspec(0), blockspec(HEAD_PAIRS), blockspec(2 * HEAD_PAIRS)],
        out_specs=blockspec(0),
        out_shape=jax.ShapeDtypeStruct((B, S, ATT_WIDTH), F32),
        scratch_shapes=[pltpu.VMEM((S, LANES), F32)] * 3,
        compiler_params=_params("parallel", "parallel"),
        name="dsa",
    )(qkv, qkv, qkv)


def _fox_kernel(q_ref, k_ref, v_ref, ft_ref, bf_ref, o_ref, qa_ref, ka_ref, *, seq, tq):
    x = ft_ref[0] + bf_ref[...]
    logf = jnp.minimum(x, 0.0) - jnp.log1p(jnp.exp(-jnp.abs(x)))
    lane_t = lax.broadcasted_iota(jnp.int32, logf.shape, 1)
    cum = logf
    shift = 1
    while shift < seq:
        cum = cum + jnp.where(lane_t >= shift, pltpu.roll(cum, shift, 1), 0.0)
        shift *= 2

    ln = lax.broadcasted_iota(jnp.int32, (seq, LANES), 1)
    row = lax.broadcasted_iota(jnp.int32, (tq, tq), 0)
    col = lax.broadcasted_iota(jnp.int32, (tq, tq), 1)
    causal = row >= col
    head0 = lax.broadcasted_iota(jnp.int32, (tq, LANES), 1) < HEAD_DIM
    one = jnp.ones((seq, LANES), F32)
    zero = jnp.zeros((seq, LANES), F32)

    for hp in range(HEAD_PAIRS):
        lanes = slice(hp * LANES, (hp + 1) * LANES)
        stacked = jnp.concatenate(
            [jnp.broadcast_to(cum[2 * hp:2 * hp + 1, :], (HEAD_DIM, seq)),
             jnp.broadcast_to(cum[2 * hp + 1:2 * hp + 2, :], (HEAD_DIM, seq))], axis=0)
        cc = stacked.T
        hi = cc.astype(BF16).astype(F32)
        r1 = cc - hi
        lo = r1.astype(BF16).astype(F32)
        lo2 = (r1 - lo).astype(BF16).astype(F32)
        qv = q_ref[0, :, lanes].astype(F32) * QK_SCALE
        kv = k_ref[0, :, lanes]
        for h in range(2):
            b0 = HEAD_DIM * h
            in_head = (ln >= b0) & (ln < b0 + HEAD_DIM)
            first3 = (ln >= b0) & (ln < b0 + 3)
            next3 = (ln >= b0 + 3) & (ln < b0 + 6)
            aq = jnp.where(ln == b0, hi, jnp.where(ln == b0 + 1, lo, jnp.where(ln == b0 + 2, lo2,
                           jnp.where(next3, one, zero))))
            ak = jnp.where(ln == b0 + 3, -hi, jnp.where(ln == b0 + 4, -lo, jnp.where(ln == b0 + 5, -lo2,
                           jnp.where(first3, one, zero))))
            qa_ref[h, :, :LANES] = jnp.where(in_head, qv, zero).astype(BF16)
            qa_ref[h, :, LANES:] = aq.astype(BF16)
            ka_ref[h, :, :LANES] = kv
            ka_ref[h, :, LANES:] = ak.astype(BF16)

        def q_block(iq, carry, lanes=lanes):
            r0 = pl.multiple_of(iq * tq, tq)
            outs = []
            for h in range(2):
                qa = qa_ref[h, pl.ds(r0, tq), :]

                def scores(c0, h=h, qa=qa):
                    return lax.dot_general(qa, ka_ref[h, pl.ds(c0, tq), :], NT_DIMS,
                                           preferred_element_type=F32)

                def update(state, s, c0, lanes=lanes):
                    m, l, acc = state
                    mn = jnp.maximum(m, jnp.max(s, axis=-1, keepdims=True))
                    alpha = jnp.exp(m - mn)
                    p = jnp.exp(s - mn)
                    l = alpha * l + jnp.sum(p, axis=-1, keepdims=True)
                    pv = jnp.dot(p.astype(BF16), v_ref[0, pl.ds(c0, tq), lanes],
                                 preferred_element_type=F32)
                    return mn, l, alpha * acc + pv

                def kv_step(j, state, scores=scores, update=update):
                    c0 = pl.multiple_of(j * tq, tq)
                    return update(state, scores(c0), c0)

                init = (jnp.full((tq, 1), NEG, F32), jnp.zeros((tq, 1), F32),
                        jnp.zeros((tq, LANES), F32))
                state = lax.fori_loop(0, iq, kv_step, init)
                m, l, acc = update(state, jnp.where(causal, scores(r0), NEG), r0)
                outs.append(acc / l)
            o_ref[0, pl.ds(r0, tq), lanes] = jnp.where(head0, outs[0], outs[1])
            return carry

        lax.fori_loop(0, seq // tq, q_block, 0)


def _fox(qkv, ft, bf, *, tq):
    B, S, _ = qkv.shape
    part = lambda off: pl.BlockSpec((1, S, ATT_WIDTH), lambda b: (b, 0, off))
    return pl.pallas_call(
        functools.partial(_fox_kernel, seq=S, tq=tq),
        grid=(B,),
        in_specs=[part(0), part(1), part(2),
                  pl.BlockSpec((1, 8, S), lambda b: (b, 0, 0)),
                  pl.BlockSpec((8, S), lambda b: (0, 0))],
        out_specs=pl.BlockSpec((1, S, ATT_WIDTH), lambda b: (b, 0, 0)),
        out_shape=jax.ShapeDtypeStruct((B, S, ATT_WIDTH), F32),
        scratch_shapes=[pltpu.VMEM((2, S, 2 * LANES), BF16)] * 2,
        compiler_params=_params("parallel"),
        name="fox",
    )(qkv, qkv, qkv, ft, bf)


def _postmix_kernel(x_ref, ys_ref, yd_ref, yf_ref, gs_ref, gd_ref, gf_ref,
                    ws_ref, wd_ref, wf_ref, gp_ref, o_ref):
    o = jnp.dot(_rms_scale(ys_ref[...], gs_ref[...]).astype(BF16), ws_ref[...],
                preferred_element_type=F32)
    o += jnp.dot(_rms_scale(yd_ref[0], gd_ref[...]).astype(BF16), wd_ref[...],
                 preferred_element_type=F32)
    o += jnp.dot(_rms_scale(yf_ref[0], gf_ref[...]).astype(BF16), wf_ref[...],
                 preferred_element_type=F32)
    o_ref[0] = x_ref[0] + _rms_scale(o, gp_ref[...])


def _postmix(x, ys_sb, yd, yf, gs, gd, gf, ws, wd, wf, gp, *, tm):
    B, S, D = x.shape
    const = lambda b, j: (0, 0)
    tok = lambda w: pl.BlockSpec((1, tm, w), lambda b, j: (b, j, 0))
    full = lambda a: pl.BlockSpec(a.shape, const)
    return pl.pallas_call(
        _postmix_kernel,
        grid=(B, S // tm),
        in_specs=[tok(D), pl.BlockSpec((tm, SSM_WIDTH), lambda b, j: (j, b)), tok(ATT_WIDTH),
                  tok(ATT_WIDTH), full(gs), full(gd), full(gf), full(ws), full(wd), full(wf),
                  full(gp)],
        out_specs=tok(D),
        out_shape=jax.ShapeDtypeStruct((B, S, D), F32),
        compiler_params=_params("parallel", "parallel"),
        name="postmix",
    )(x, ys_sb, yd, yf, gs, gd, gf, ws, wd, wf, gp)


def _ffn_kernel(x_ref, g1_ref, wg_ref, wu_ref, wd_ref, g2_ref, o_ref, *, f_chunk):
    x = x_ref[...]
    h = _rms_scale(x, g1_ref[...]).astype(BF16)
    d_ff = wg_ref.shape[1]
    acc = jnp.zeros(x.shape, F32)
    for c in range(d_ff // f_chunk):
        cols = slice(c * f_chunk, (c + 1) * f_chunk)
        gate = jnp.dot(h, wg_ref[:, cols], preferred_element_type=F32)
        up = jnp.dot(h, wu_ref[:, cols], preferred_element_type=F32)
        act = (gate * jax.nn.sigmoid(gate) * up).astype(BF16)
        acc += jnp.dot(act, wd_ref[cols, :], preferred_element_type=F32)
    o_ref[...] = x + _rms_scale(acc, g2_ref[...])


def _ffn(x2d, g1, wg, wu, wd, g2, *, tm, f_chunk):
    T, D = x2d.shape
    const = lambda i: (0, 0)
    full = lambda a: pl.BlockSpec(a.shape, const)
    return pl.pallas_call(
        functools.partial(_ffn_kernel, f_chunk=f_chunk),
        grid=(T // tm,),
        in_specs=[pl.BlockSpec((tm, D), lambda i: (i, 0)), full(g1), full(wg), full(wu), full(wd),
                  full(g2)],
        out_specs=pl.BlockSpec((tm, D), lambda i: (i, 0)),
        out_shape=jax.ShapeDtypeStruct((T, D), F32),
        compiler_params=_params("parallel"),
        name="ffn",
    )(x2d, g1, wg, wu, wd, g2)


def _s5_matrices(log_dt, a_re, a_im, b_re, b_im, c_re, c_im):
    G, P = a_re.shape
    C = b_re.shape[-1]
    dt = jnp.exp(log_dt)[:, None]
    decay = jnp.exp(a_re * dt)
    abar_re = decay * jnp.cos(a_im * dt)
    abar_im = decay * jnp.sin(a_im * dt)
    mag2 = a_re * a_re + a_im * a_im
    coef_re = ((abar_re - 1.0) * a_re + abar_im * a_im) / mag2
    coef_im = (abar_im * a_re - (abar_re - 1.0) * a_im) / mag2
    bbar_re = coef_re[..., None] * b_re - coef_im[..., None] * b_im
    bbar_im = coef_re[..., None] * b_im + coef_im[..., None] * b_re
    eye = jnp.eye(G, dtype=F32)
    bt_re = jnp.einsum('gpc,gh->gchp', bbar_re, eye).reshape(G * C, G * P)
    bt_im = jnp.einsum('gpc,gh->gchp', bbar_im, eye).reshape(G * C, G * P)
    bt = jnp.concatenate([bt_re, bt_im], axis=1).astype(BF16)
    ct_re = jnp.einsum('gcp,gh->gphc', c_re, eye).reshape(G * P, G * C)
    ct_im = jnp.einsum('gcp,gh->gphc', -c_im, eye).reshape(G * P, G * C)
    ct = jnp.concatenate([ct_re, ct_im], axis=0).astype(BF16)
    return bt, abar_re.reshape(1, G * P), abar_im.reshape(1, G * P), ct


def kernel(x, g_pre_mix, w_in, b_f, ssm_log_dt, ssm_a_re, ssm_a_im, ssm_b_re, ssm_b_im, ssm_c_re, ssm_c_im, ssm_d, w_glu, b_glu, g_mix, w_out, g_post_mix, g_pre_ffn, w_gate_up, w_down, g_post_ffn):
    B, S, D = x.shape
    depth = w_in.shape[0]
    d_ff = w_down.shape[1]
    tm = 512
    o1 = SSM_WIDTH
    o2 = o1 + 3 * ATT_WIDTH
    o3 = o2 + 3 * ATT_WIDTH
    row = lambda v: v.reshape(1, -1)

    for l in range(depth):
        w = w_in[l]
        wft = jnp.zeros((8, D), F32).at[:N_HEADS].set(w[:, o3:].T).astype(BF16)
        u_sb, qkv_dsa, qkv_fox, ft = _premix(
            x, row(g_pre_mix[l]), w[:, :o1].astype(BF16), w[:, o1:o2].astype(BF16),
            w[:, o2:o3].astype(BF16), wft, tm=tm)

        bt, a_re, a_im, ct = _s5_matrices(ssm_log_dt[l], ssm_a_re[l], ssm_a_im[l], ssm_b_re[l],
                                          ssm_b_im[l], ssm_c_re[l], ssm_c_im[l])
        y_ssm = _s5(u_sb.reshape(S * B, SSM_WIDTH), bt, a_re, a_im, ct, row(ssm_d[l]),
                    w_glu[l].astype(BF16), row(b_glu[l]), batch=B, steps=32)

        y_dsa = _dsa(qkv_dsa)

        bf = jnp.zeros((8,), F32).at[:N_HEADS].set(b_f[l])
        y_fox = _fox(qkv_fox, ft, jnp.broadcast_to(bf[:, None], (8, S)), tq=256)

        gm = g_mix[l]
        wo = w_out[l].astype(BF16)
        x = _postmix(x, y_ssm.reshape(S, B * SSM_WIDTH), y_dsa, y_fox,
                     row(gm[:o1]), row(gm[o1:o1 + ATT_WIDTH]), row(gm[o1 + ATT_WIDTH:]),
                     wo[:o1], wo[o1:o1 + ATT_WIDTH], wo[o1 + ATT_WIDTH:], row(g_post_mix[l]), tm=tm)

        wgu = w_gate_up[l].astype(BF16)
        x = _ffn(x.reshape(B * S, D), row(g_pre_ffn[l]), wgu[:, :d_ff], wgu[:, d_ff:],
                 w_down[l].astype(BF16), row(g_post_ffn[l]), tm=tm, f_chunk=d_ff // 2
                 ).reshape(B, S, D)
    return x
```

```python
import functools

import jax
import jax.numpy as jnp
from jax import lax
from jax.experimental import pallas as pl
from jax.experimental.pallas import tpu as pltpu

F32 = jnp.float32
BF16 = jnp.bfloat16

HEAD_DIM = 64
SSM_WIDTH = 256
ATT_WIDTH = 384
N_HEADS = 6
HEAD_PAIRS = N_HEADS // 2
DSA_SPAN = 128
DSA_BLOCK = 128
DSA_DILATIONS = (1, 4, 16)
RMS_EPS = 1e-6
NEG = -1e30
QK_SCALE = HEAD_DIM ** -0.5

LANES = 128
BF16_SUBLANES = 16
VMEM_LIMIT = 56 * 1024 * 1024

NT_DIMS = (((1,), (1,)), ((), ()))


def _rms_scale(x, g):
    ms = jnp.mean(x * x, axis=-1, keepdims=True)
    return x * lax.rsqrt(ms + RMS_EPS) * g


def _params(*sem):
    return pltpu.CompilerParams(dimension_semantics=sem, vmem_limit_bytes=VMEM_LIMIT)


def _premix_kernel(x_ref, g_ref, wu_ref, wd_ref, wqk_ref, wvt_ref, wft_ref,
                   u_ref, qd_ref, qk_ref, vt_ref, ft_ref):
    h = _rms_scale(x_ref[0], g_ref[...]).astype(BF16)
    u_ref[...] = jnp.dot(h, wu_ref[...], preferred_element_type=F32)
    qd_ref[0] = jnp.dot(h, wd_ref[...], preferred_element_type=F32)
    qk_ref[0] = jnp.dot(h, wqk_ref[...], preferred_element_type=F32).astype(BF16)
    vt_ref[0] = lax.dot_general(wvt_ref[...], h, NT_DIMS, preferred_element_type=F32).astype(BF16)
    ft_ref[0] = lax.dot_general(wft_ref[...], h, NT_DIMS, preferred_element_type=F32)


def _premix(x, g, wu, wd, wqk, wvt, wft, *, tm):
    B, S, D = x.shape
    grid = (B, S // tm)
    const = lambda b, j: (0, 0)
    full = lambda a: pl.BlockSpec(a.shape, const)
    return pl.pallas_call(
        _premix_kernel,
        grid=grid,
        in_specs=[pl.BlockSpec((1, tm, D), lambda b, j: (b, j, 0)), full(g), full(wu), full(wd),
                  full(wqk), full(wvt), full(wft)],
        out_specs=[
            pl.BlockSpec((tm, SSM_WIDTH), lambda b, j: (j, b)),
            pl.BlockSpec((1, tm, 3 * ATT_WIDTH), lambda b, j: (b, j, 0)),
            pl.BlockSpec((1, tm, 2 * ATT_WIDTH), lambda b, j: (b, j, 0)),
            pl.BlockSpec((1, ATT_WIDTH, tm), lambda b, j: (b, 0, j)),
            pl.BlockSpec((1, 8 * HEAD_PAIRS, tm), lambda b, j: (b, 0, j)),
        ],
        out_shape=[
            jax.ShapeDtypeStruct((S, B * SSM_WIDTH), F32),
            jax.ShapeDtypeStruct((B, S, 3 * ATT_WIDTH), F32),
            jax.ShapeDtypeStruct((B, S, 2 * ATT_WIDTH), BF16),
            jax.ShapeDtypeStruct((B, ATT_WIDTH, S), BF16),
            jax.ShapeDtypeStruct((B, 8 * HEAD_PAIRS, S), F32),
        ],
        compiler_params=_params("parallel", "parallel"),
        name="premix",
    )(x, g, wu, wd, wqk, wvt, wft)


def _s5_kernel(u_ref, bt_ref, are_ref, aim_ref, ct_ref, d_ref, wg_ref, bg_ref, y_ref,
               xs_ref, h_ref, *, batch, steps, col_tile):
    n_state = are_ref.shape[1]

    @pl.when(pl.program_id(0) == 0)
    def _():
        h_ref[...] = jnp.zeros_like(h_ref)

    u = u_ref[...]
    xs_ref[...] = jnp.dot(u.astype(BF16), bt_ref[...], preferred_element_type=F32)

    for c in range(n_state // col_tile):
        re = slice(c * col_tile, (c + 1) * col_tile)
        im = slice(n_state + c * col_tile, n_state + (c + 1) * col_tile)
        ar = are_ref[:, re]
        ai = aim_ref[:, re]

        def step(s, carry, re=re, im=im, ar=ar, ai=ai):
            hr, hi = carry
            rows = pl.ds(pl.multiple_of(s * batch, batch), batch)
            nr = ar * hr - ai * hi + xs_ref[rows, re]
            ni = ar * hi + ai * hr + xs_ref[rows, im]
            xs_ref[rows, re] = nr
            xs_ref[rows, im] = ni
            return nr, ni

        hr, hi = lax.fori_loop(0, steps, step, (h_ref[:, re], h_ref[:, im]), unroll=2)
        h_ref[:, re] = hr
        h_ref[:, im] = hi

    y = jnp.dot(xs_ref[...].astype(BF16), ct_ref[...], preferred_element_type=F32)
    y = y + d_ref[...] * u
    g = jax.nn.gelu(y)
    z = jnp.dot(g.astype(BF16), wg_ref[...], preferred_element_type=F32) + bg_ref[...]
    y_ref[...] = g * jax.nn.sigmoid(z)


def _s5(u_sb, bt, a_re, a_im, ct, d, wg, bg, *, batch, steps):
    rows_total, width = u_sb.shape
    rows = steps * batch
    n_cols = bt.shape[1]
    const = lambda i: (0, 0)
    kern = functools.partial(_s5_kernel, batch=batch, steps=steps, col_tile=2 * LANES)
    return pl.pallas_call(
        kern,
        grid=(rows_total // rows,),
        in_specs=[
            pl.BlockSpec((rows, width), lambda i: (i, 0)),
            pl.BlockSpec(bt.shape, const),
            pl.BlockSpec(a_re.shape, const),
            pl.BlockSpec(a_im.shape, const),
            pl.BlockSpec(ct.shape, const),
            pl.BlockSpec(d.shape, const),
            pl.BlockSpec(wg.shape, const),
            pl.BlockSpec(bg.shape, const),
        ],
        out_specs=pl.BlockSpec((rows, width), lambda i: (i, 0)),
        out_shape=jax.ShapeDtypeStruct((rows_total, width), F32),
        scratch_shapes=[
            pltpu.VMEM((rows, n_cols), F32),
            pltpu.VMEM((batch, n_cols), F32),
        ],
        compiler_params=_params("arbitrary"),
        name="s5",
    )(u_sb, bt, a_re, a_im, ct, d, wg, bg)


def _dsa_attend(qb, kw, vw, valid, head0):
    vx = jnp.concatenate([vw, jnp.ones_like(vw)], axis=1)
    parts = []
    for h in range(2):
        qh = jnp.where(head0 if h == 0 else jnp.logical_not(head0), qb, jnp.zeros_like(qb))
        s = lax.dot_general(qh, kw, NT_DIMS, preferred_element_type=F32)
        s = jnp.where(valid, s, NEG)
        m = jnp.max(s, axis=-1, keepdims=True)
        p = jnp.exp(s - m).astype(BF16)
        r = jnp.dot(p, vx, preferred_element_type=F32)
        parts.append((r[:, :LANES], jnp.broadcast_to(m, (qb.shape[0], LANES)), r[:, LANES:]))
    return tuple(jnp.where(head0, a, b) for a, b in zip(*parts))


def _softmax_merge(n1, m1, l1, n2, m2, l2):
    m = jnp.maximum(m1, m2)
    e1 = jnp.exp(m1 - m)
    e2 = jnp.exp(m2 - m)
    return n1 * e1 + n2 * e2, m, l1 * e1 + l2 * e2


def _dsa_kernel(q_ref, k_ref, v_ref, o_ref, acc_ref, m_ref, l_ref, *, seq):
    blk = DSA_BLOCK
    n_blocks = seq // blk
    head0 = lax.broadcasted_iota(jnp.int32, (blk, LANES), 1) < HEAD_DIM
    qi = lax.broadcasted_iota(jnp.int32, (blk, 2 * blk), 0)
    ki = lax.broadcasted_iota(jnp.int32, (blk, 2 * blk), 1)
    band = (ki >= qi) & (ki <= qi + DSA_SPAN)
    band_first = band & (ki >= blk)
    causal = (lax.broadcasted_iota(jnp.int32, (blk, blk), 1)
              <= lax.broadcasted_iota(jnp.int32, (blk, blk), 0))

    def rows(start, d):
        return pl.ds(start, blk) if d == 1 else pl.ds(start, blk, stride=d)

    def load(ref, start, d, scale=None):
        x = ref[0, rows(start, d), :]
        return (x if scale is None else x * scale).astype(BF16)

    def banded(r, n, d):
        start = r + d * blk * n
        prev = r + d * blk * max(n - 1, 0)
        kw = jnp.concatenate([load(k_ref, prev, d), load(k_ref, start, d)], axis=0)
        vw = jnp.concatenate([load(v_ref, prev, d), load(v_ref, start, d)], axis=0)
        valid = band if n > 0 else band_first
        return _dsa_attend(load(q_ref, start, d, QK_SCALE), kw, vw, valid, head0)

    def state_rows(r, n, d):
        return rows(r + d * blk * n, d)

    for n in range(n_blocks):
        num, mx, den = banded(0, n, 1)
        rs = state_rows(0, n, 1)
        acc_ref[rs, :] = num
        m_ref[rs, :] = mx
        l_ref[rs, :] = den

    d_mid = DSA_DILATIONS[1]

    def pattern_mid(r, carry):
        per_res = n_blocks // d_mid
        new = [banded(r, n, d_mid) for n in range(per_res)]
        old = [(acc_ref[state_rows(r, n, d_mid), :], m_ref[state_rows(r, n, d_mid), :],
                l_ref[state_rows(r, n, d_mid), :]) for n in range(per_res)]
        for n in range(per_res):
            num, mx, den = _softmax_merge(*old[n], *new[n])
            rs = state_rows(r, n, d_mid)
            acc_ref[rs, :] = num
            m_ref[rs, :] = mx
            l_ref[rs, :] = den
        return carry

    lax.fori_loop(0, d_mid, pattern_mid, 0)

    d_wide = DSA_DILATIONS[2]
    group = 4

    def pattern_wide(it, carry):
        base = it * group
        new = []
        for u in range(group):
            rs = rows(base + u, d_wide)
            new.append(_dsa_attend(load(q_ref, base + u, d_wide, QK_SCALE), load(k_ref, base + u, d_wide),
                                   load(v_ref, base + u, d_wide), causal, head0))
        old = [(acc_ref[rows(base + u, d_wide), :], m_ref[rows(base + u, d_wide), :],
                l_ref[rows(base + u, d_wide), :]) for u in range(group)]
        for u in range(group):
            num, _, den = _softmax_merge(*old[u], *new[u])
            o_ref[0, rows(base + u, d_wide), :] = num / den
        return carry

    lax.fori_loop(0, d_wide // group, pattern_wide, 0)


def _dsa(qkv):
    B, S, _ = qkv.shape
    assert S == DSA_DILATIONS[2] * DSA_BLOCK, S
    blockspec = lambda off: pl.BlockSpec((1, S, LANES), lambda b, p: (b, 0, off + p))
    return pl.pallas_call(
        functools.partial(_dsa_kernel, seq=S),
        grid=(B, HEAD_PAIRS),
        in_specs=[blockspec(0), blockspec(HEAD_PAIRS), blockspec(2 * HEAD_PAIRS)],
        out_specs=blockspec(0),
        out_shape=jax.ShapeDtypeStruct((B, S, ATT_WIDTH), F32),
        scratch_shapes=[pltpu.VMEM((S, LANES), F32)] * 3,
        compiler_params=_params("parallel", "parallel"),
        name="dsa",
    )(qkv, qkv, qkv)


def _fox_kernel(q_ref, k_ref, vt_ref, ft_ref, bf_ref, o_ref, qa_ref, ka_ref, vx_ref, *, seq, tq):
    x = ft_ref[0] + bf_ref[...]
    logf = jnp.minimum(x, 0.0) - jnp.log1p(jnp.exp(-jnp.abs(x)))
    lane_t = lax.broadcasted_iota(jnp.int32, logf.shape, 1)
    cum = logf
    shift = 1
    while shift < seq:
        cum = cum + jnp.where(lane_t >= shift, pltpu.roll(cum, shift, 1), 0.0)
        shift *= 2

    stacked = jnp.concatenate([jnp.broadcast_to(cum[0:1, :], (HEAD_DIM, seq)),
                               jnp.broadcast_to(cum[1:2, :], (HEAD_DIM, seq))], axis=0)
    cc = stacked.T
    hi = cc.astype(BF16).astype(F32)
    r1 = cc - hi
    lo = r1.astype(BF16).astype(F32)
    lo2 = (r1 - lo).astype(BF16).astype(F32)
    ln = lax.broadcasted_iota(jnp.int32, (seq, LANES), 1)
    one = jnp.ones((seq, LANES), F32)
    zero = jnp.zeros((seq, LANES), F32)
    qv = q_ref[0].astype(F32) * QK_SCALE
    kv = k_ref[0]
    for h in range(2):
        b0 = HEAD_DIM * h
        in_head = (ln >= b0) & (ln < b0 + HEAD_DIM)
        first3 = (ln >= b0) & (ln < b0 + 3)
        next3 = (ln >= b0 + 3) & (ln < b0 + 6)
        aq = jnp.where(ln == b0, hi, jnp.where(ln == b0 + 1, lo, jnp.where(ln == b0 + 2, lo2,
                       jnp.where(next3, one, zero))))
        ak = jnp.where(ln == b0 + 3, -hi, jnp.where(ln == b0 + 4, -lo, jnp.where(ln == b0 + 5, -lo2,
                       jnp.where(first3, one, zero))))
        qa_ref[h, :, :LANES] = jnp.where(in_head, qv, zero).astype(BF16)
        qa_ref[h, :, LANES:] = aq.astype(BF16)
        ka_ref[h, :, :LANES] = kv
        ka_ref[h, :, LANES:] = ak.astype(BF16)

    vx_ref[:LANES, :] = vt_ref[0]
    vx_ref[LANES:, :] = jnp.ones((BF16_SUBLANES, seq), BF16)

    key_le_query = (lax.broadcasted_iota(jnp.int32, (tq, tq), 0)
                    <= lax.broadcasted_iota(jnp.int32, (tq, tq), 1))

    for iq in range(seq // tq):
        q0 = iq * tq
        outs = []
        for h in range(2):
            qa = qa_ref[h, q0:q0 + tq, :]
            s_dg = lax.dot_general(ka_ref[h, q0:q0 + tq, :], qa, NT_DIMS, preferred_element_type=F32)
            s_dg = jnp.where(key_le_query, s_dg, NEG)
            m = jnp.max(s_dg, axis=0, keepdims=True)
            if iq > 0:
                s_off = lax.dot_general(ka_ref[h, :q0, :], qa, NT_DIMS, preferred_element_type=F32)
                m = jnp.maximum(m, jnp.max(s_off, axis=0, keepdims=True))
            acc = jnp.dot(vx_ref[:, q0:q0 + tq], jnp.exp(s_dg - m).astype(BF16),
                          preferred_element_type=F32)
            if iq > 0:
                acc += jnp.dot(vx_ref[:, :q0], jnp.exp(s_off - m).astype(BF16),
                               preferred_element_type=F32)
            outs.append(acc[HEAD_DIM * h:HEAD_DIM * (h + 1), :] / acc[LANES:LANES + 1, :])
        o_ref[0, q0:q0 + tq, :] = jnp.concatenate(outs, axis=0).T


def _fox(qk, vt, ft, bf, *, tq):
    B, S, _ = qk.shape
    tok = lambda off: pl.BlockSpec((1, S, LANES), lambda b, p: (b, 0, off + p))
    feat = lambda rows: pl.BlockSpec((1, rows, S), lambda b, p: (b, p, 0))
    return pl.pallas_call(
        functools.partial(_fox_kernel, seq=S, tq=tq),
        grid=(B, HEAD_PAIRS),
        in_specs=[tok(0), tok(HEAD_PAIRS), feat(LANES), feat(8),
                  pl.BlockSpec((8, S), lambda b, p: (p, 0))],
        out_specs=tok(0),
        out_shape=jax.ShapeDtypeStruct((B, S, ATT_WIDTH), F32),
        scratch_shapes=[pltpu.VMEM((2, S, 2 * LANES), BF16), pltpu.VMEM((2, S, 2 * LANES), BF16),
                        pltpu.VMEM((LANES + BF16_SUBLANES, S), BF16)],
        compiler_params=_params("parallel", "parallel"),
        name="fox",
    )(qk, qk, vt, ft, bf)


def _postmix_kernel(x_ref, ys_ref, yd_ref, yf_ref, gs_ref, gd_ref, gf_ref,
                    ws_ref, wd_ref, wf_ref, gp_ref, o_ref):
    o = jnp.dot(_rms_scale(ys_ref[...], gs_ref[...]).astype(BF16), ws_ref[...],
                preferred_element_type=F32)
    o += jnp.dot(_rms_scale(yd_ref[0], gd_ref[...]).astype(BF16), wd_ref[...],
                 preferred_element_type=F32)
    o += jnp.dot(_rms_scale(yf_ref[0], gf_ref[...]).astype(BF16), wf_ref[...],
                 preferred_element_type=F32)
    o_ref[0] = x_ref[0] + _rms_scale(o, gp_ref[...])


def _postmix(x, ys_sb, yd, yf, gs, gd, gf, ws, wd, wf, gp, *, tm):
    B, S, D = x.shape
    const = lambda b, j: (0, 0)
    tok = lambda w: pl.BlockSpec((1, tm, w), lambda b, j: (b, j, 0))
    full = lambda a: pl.BlockSpec(a.shape, const)
    return pl.pallas_call(
        _postmix_kernel,
        grid=(B, S // tm),
        in_specs=[tok(D), pl.BlockSpec((tm, SSM_WIDTH), lambda b, j: (j, b)), tok(ATT_WIDTH),
                  tok(ATT_WIDTH), full(gs), full(gd), full(gf), full(ws), full(wd), full(wf),
                  full(gp)],
        out_specs=tok(D),
        out_shape=jax.ShapeDtypeStruct((B, S, D), F32),
        compiler_params=_params("parallel", "parallel"),
        name="postmix",
    )(x, ys_sb, yd, yf, gs, gd, gf, ws, wd, wf, gp)


def _ffn_kernel(x_ref, g1_ref, wg_ref, wu_ref, wd_ref, g2_ref, o_ref, *, f_chunk):
    x = x_ref[...]
    h = _rms_scale(x, g1_ref[...]).astype(BF16)
    d_ff = wg_ref.shape[1]
    acc = jnp.zeros(x.shape, F32)
    for c in range(d_ff // f_chunk):
        cols = slice(c * f_chunk, (c + 1) * f_chunk)
        gate = jnp.dot(h, wg_ref[:, cols], preferred_element_type=F32)
        up = jnp.dot(h, wu_ref[:, cols], preferred_element_type=F32)
        act = (gate * jax.nn.sigmoid(gate) * up).astype(BF16)
        acc += jnp.dot(act, wd_ref[cols, :], preferred_element_type=F32)
    o_ref[...] = x + _rms_scale(acc, g2_ref[...])


def _ffn(x2d, g1, wg, wu, wd, g2, *, tm, f_chunk):
    T, D = x2d.shape
    const = lambda i: (0, 0)
    full = lambda a: pl.BlockSpec(a.shape, const)
    return pl.pallas_call(
        functools.partial(_ffn_kernel, f_chunk=f_chunk),
        grid=(T // tm,),
        in_specs=[pl.BlockSpec((tm, D), lambda i: (i, 0)), full(g1), full(wg), full(wu), full(wd),
                  full(g2)],
        out_specs=pl.BlockSpec((tm, D), lambda i: (i, 0)),
        out_shape=jax.ShapeDtypeStruct((T, D), F32),
        compiler_params=_params("parallel"),
        name="ffn",
    )(x2d, g1, wg, wu, wd, g2)


def _s5_matrices(log_dt, a_re, a_im, b_re, b_im, c_re, c_im):
    G, P = a_re.shape
    C = b_re.shape[-1]
    dt = jnp.exp(log_dt)[:, None]
    decay = jnp.exp(a_re * dt)
    abar_re = decay * jnp.cos(a_im * dt)
    abar_im = decay * jnp.sin(a_im * dt)
    mag2 = a_re * a_re + a_im * a_im
    coef_re = ((abar_re - 1.0) * a_re + abar_im * a_im) / mag2
    coef_im = (abar_im * a_re - (abar_re - 1.0) * a_im) / mag2
    bbar_re = coef_re[..., None] * b_re - coef_im[..., None] * b_im
    bbar_im = coef_re[..., None] * b_im + coef_im[..., None] * b_re
    eye = jnp.eye(G, dtype=F32)
    bt_re = jnp.einsum('gpc,gh->gchp', bbar_re, eye).reshape(G * C, G * P)
    bt_im = jnp.einsum('gpc,gh->gchp', bbar_im, eye).reshape(G * C, G * P)
    bt = jnp.concatenate([bt_re, bt_im], axis=1).astype(BF16)
    ct_re = jnp.einsum('gcp,gh->gphc', c_re, eye).reshape(G * P, G * C)
    ct_im = jnp.einsum('gcp,gh->gphc', -c_im, eye).reshape(G * P, G * C)
    ct = jnp.concatenate([ct_re, ct_im], axis=0).astype(BF16)
    return bt, abar_re.reshape(1, G * P), abar_im.reshape(1, G * P), ct


def _per_pair_rows(v):
    out = jnp.zeros((HEAD_PAIRS, 8) + v.shape[1:], v.dtype)
    out = out.at[:, :2].set(v.reshape((HEAD_PAIRS, 2) + v.shape[1:]))
    return out.reshape((HEAD_PAIRS * 8,) + v.shape[1:])


def kernel(x, g_pre_mix, w_in, b_f, ssm_log_dt, ssm_a_re, ssm_a_im, ssm_b_re, ssm_b_im, ssm_c_re, ssm_c_im, ssm_d, w_glu, b_glu, g_mix, w_out, g_post_mix, g_pre_ffn, w_gate_up, w_down, g_post_ffn):
    B, S, D = x.shape
    depth = w_in.shape[0]
    d_ff = w_down.shape[1]
    tm = 512
    o1 = SSM_WIDTH
    o2 = o1 + 3 * ATT_WIDTH
    o3 = o2 + 2 * ATT_WIDTH
    o4 = o3 + ATT_WIDTH
    row = lambda v: v.reshape(1, -1)

    for l in range(depth):
        w = w_in[l]
        u_sb, qkv_dsa, qk_fox, vt_fox, ft = _premix(
            x, row(g_pre_mix[l]), w[:, :o1].astype(BF16), w[:, o1:o2].astype(BF16),
            w[:, o2:o3].astype(BF16), w[:, o3:o4].T.astype(BF16),
            _per_pair_rows(w[:, o4:].T).astype(BF16), tm=tm)

        bt, a_re, a_im, ct = _s5_matrices(ssm_log_dt[l], ssm_a_re[l], ssm_a_im[l], ssm_b_re[l],
                                          ssm_b_im[l], ssm_c_re[l], ssm_c_im[l])
        y_ssm = _s5(u_sb.reshape(S * B, SSM_WIDTH), bt, a_re, a_im, ct, row(ssm_d[l]),
                    w_glu[l].astype(BF16), row(b_glu[l]), batch=B, steps=32)

        y_dsa = _dsa(qkv_dsa)

        bf = jnp.broadcast_to(_per_pair_rows(b_f[l][:, None]), (8 * HEAD_PAIRS, S))
        y_fox = _fox(qk_fox, vt_fox, ft, bf, tq=256)

        gm = g_mix[l]
        wo = w_out[l].astype(BF16)
        x = _postmix(x, y_ssm.reshape(S, B * SSM_WIDTH), y_dsa, y_fox,
                     row(gm[:o1]), row(gm[o1:o1 + ATT_WIDTH]), row(gm[o1 + ATT_WIDTH:]),
                     wo[:o1], wo[o1:o1 + ATT_WIDTH], wo[o1 + ATT_WIDTH:], row(g_post_mix[l]), tm=tm)

        wgu = w_gate_up[l].astype(BF16)
        x = _ffn(x.reshape(B * S, D), row(g_pre_ffn[l]), wgu[:, :d_ff], wgu[:, d_ff:],
                 w_down[l].astype(BF16), row(g_post_ffn[l]), tm=tm, f_chunk=d_ff // 2
                 ).reshape(B, S, D)
    return x
```

```python
import functools

import jax
import jax.numpy as jnp
from jax import lax
from jax.experimental import pallas as pl
from jax.experimental.pallas import tpu as pltpu

F32 = jnp.float32
BF16 = jnp.bfloat16

HEAD_DIM = 64
SSM_WIDTH = 256
ATT_WIDTH = 384
N_HEADS = 6
HEAD_PAIRS = N_HEADS // 2
DSA_SPAN = 128
DSA_BLOCK = 128
DSA_DILATIONS = (1, 4, 16)
RMS_EPS = 1e-6
NEG = -1e30
QK_SCALE = HEAD_DIM ** -0.5

LANES = 128
BF16_SUBLANES = 16
VMEM_LIMIT = 56 * 1024 * 1024

NT_DIMS = (((1,), (1,)), ((), ()))


def _rms_scale(x, g):
    ms = jnp.mean(x * x, axis=-1, keepdims=True)
    return x * lax.rsqrt(ms + RMS_EPS) * g


def _params(*sem):
    return pltpu.CompilerParams(dimension_semantics=sem, vmem_limit_bytes=VMEM_LIMIT)


def _premix_kernel(x_ref, g_ref, wu_ref, wd_ref, wqk_ref, wvt_ref, wft_ref,
                   u_ref, qd_ref, qk_ref, vt_ref, ft_ref):
    h = _rms_scale(x_ref[0], g_ref[...]).astype(BF16)
    u_ref[...] = jnp.dot(h, wu_ref[...], preferred_element_type=F32)
    qd_ref[0] = jnp.dot(h, wd_ref[...], preferred_element_type=F32)
    qk_ref[0] = jnp.dot(h, wqk_ref[...], preferred_element_type=F32).astype(BF16)
    vt_ref[0] = lax.dot_general(wvt_ref[...], h, NT_DIMS, preferred_element_type=F32).astype(BF16)
    ft_ref[0] = lax.dot_general(wft_ref[...], h, NT_DIMS, preferred_element_type=F32)


def _premix(x, g, wu, wd, wqk, wvt, wft, *, tm):
    B, S, D = x.shape
    grid = (B, S // tm)
    const = lambda b, j: (0, 0)
    full = lambda a: pl.BlockSpec(a.shape, const)
    return pl.pallas_call(
        _premix_kernel,
        grid=grid,
        in_specs=[pl.BlockSpec((1, tm, D), lambda b, j: (b, j, 0)), full(g), full(wu), full(wd),
                  full(wqk), full(wvt), full(wft)],
        out_specs=[
            pl.BlockSpec((tm, SSM_WIDTH), lambda b, j: (j, b)),
            pl.BlockSpec((1, tm, 3 * ATT_WIDTH), lambda b, j: (b, j, 0)),
            pl.BlockSpec((1, tm, 2 * ATT_WIDTH), lambda b, j: (b, j, 0)),
            pl.BlockSpec((1, ATT_WIDTH, tm), lambda b, j: (b, 0, j)),
            pl.BlockSpec((1, 8 * HEAD_PAIRS, tm), lambda b, j: (b, 0, j)),
        ],
        out_shape=[
            jax.ShapeDtypeStruct((S, B * SSM_WIDTH), F32),
            jax.ShapeDtypeStruct((B, S, 3 * ATT_WIDTH), F32),
            jax.ShapeDtypeStruct((B, S, 2 * ATT_WIDTH), BF16),
            jax.ShapeDtypeStruct((B, ATT_WIDTH, S), BF16),
            jax.ShapeDtypeStruct((B, 8 * HEAD_PAIRS, S), F32),
        ],
        compiler_params=_params("parallel", "parallel"),
        name="premix",
    )(x, g, wu, wd, wqk, wvt, wft)


def _s5_kernel(u_ref, bt_ref, are_ref, aim_ref, ct_ref, d_ref, wg_ref, bg_ref, y_ref,
               xs_ref, h_ref, *, batch, steps, col_tile):
    n_state = are_ref.shape[1]

    @pl.when(pl.program_id(0) == 0)
    def _():
        h_ref[...] = jnp.zeros_like(h_ref)

    u = u_ref[...]
    xs_ref[...] = jnp.dot(u.astype(BF16), bt_ref[...], preferred_element_type=F32)

    for c in range(n_state // col_tile):
        re = slice(c * col_tile, (c + 1) * col_tile)
        im = slice(n_state + c * col_tile, n_state + (c + 1) * col_tile)
        ar = are_ref[:, re]
        ai = aim_ref[:, re]

        def step(s, carry, re=re, im=im, ar=ar, ai=ai):
            hr, hi = carry
            rows = pl.ds(pl.multiple_of(s * batch, batch), batch)
            nr = ar * hr - ai * hi + xs_ref[rows, re]
            ni = ar * hi + ai * hr + xs_ref[rows, im]
            xs_ref[rows, re] = nr
            xs_ref[rows, im] = ni
            return nr, ni

        hr, hi = lax.fori_loop(0, steps, step, (h_ref[:, re], h_ref[:, im]), unroll=2)
        h_ref[:, re] = hr
        h_ref[:, im] = hi

    y = jnp.dot(xs_ref[...].astype(BF16), ct_ref[...], preferred_element_type=F32)
    y = y + d_ref[...] * u
    g = jax.nn.gelu(y)
    z = jnp.dot(g.astype(BF16), wg_ref[...], preferred_element_type=F32) + bg_ref[...]
    y_ref[...] = g * jax.nn.sigmoid(z)


def _s5(u_sb, bt, a_re, a_im, ct, d, wg, bg, *, batch, steps):
    rows_total, width = u_sb.shape
    rows = steps * batch
    n_cols = bt.shape[1]
    const = lambda i: (0, 0)
    kern = functools.partial(_s5_kernel, batch=batch, steps=steps, col_tile=2 * LANES)
    return pl.pallas_call(
        kern,
        grid=(rows_total // rows,),
        in_specs=[
            pl.BlockSpec((rows, width), lambda i: (i, 0)),
            pl.BlockSpec(bt.shape, const),
            pl.BlockSpec(a_re.shape, const),
            pl.BlockSpec(a_im.shape, const),
            pl.BlockSpec(ct.shape, const),
            pl.BlockSpec(d.shape, const),
            pl.BlockSpec(wg.shape, const),
            pl.BlockSpec(bg.shape, const),
        ],
        out_specs=pl.BlockSpec((rows, width), lambda i: (i, 0)),
        out_shape=jax.ShapeDtypeStruct((rows_total, width), F32),
        scratch_shapes=[
            pltpu.VMEM((rows, n_cols), F32),
            pltpu.VMEM((batch, n_cols), F32),
        ],
        compiler_params=_params("arbitrary"),
        name="s5",
    )(u_sb, bt, a_re, a_im, ct, d, wg, bg)


def _dsa_attend(qb, kw, vw, valid, head0):
    vx = jnp.concatenate([vw, jnp.ones_like(vw)], axis=1)
    parts = []
    for h in range(2):
        qh = jnp.where(head0 if h == 0 else jnp.logical_not(head0), qb, jnp.zeros_like(qb))
        s = lax.dot_general(qh, kw, NT_DIMS, preferred_element_type=F32)
        s = jnp.where(valid, s, NEG)
        m = jnp.max(s, axis=-1, keepdims=True)
        p = jnp.exp(s - m).astype(BF16)
        r = jnp.dot(p, vx, preferred_element_type=F32)
        parts.append((r[:, :LANES], jnp.broadcast_to(m, (qb.shape[0], LANES)), r[:, LANES:]))
    return tuple(jnp.where(head0, a, b) for a, b in zip(*parts))


def _softmax_merge(n1, m1, l1, n2, m2, l2):
    m = jnp.maximum(m1, m2)
    e1 = jnp.exp(m1 - m)
    e2 = jnp.exp(m2 - m)
    return n1 * e1 + n2 * e2, m, l1 * e1 + l2 * e2


def _dsa_kernel(q_ref, k_ref, v_ref, o_ref, acc_ref, m_ref, l_ref,
                q4_ref, k4_ref, v4_ref, acc4_ref, m4_ref, l4_ref, *, seq):
    blk = DSA_BLOCK
    n_blocks = seq // blk
    head0 = lax.broadcasted_iota(jnp.int32, (blk, LANES), 1) < HEAD_DIM
    qi = lax.broadcasted_iota(jnp.int32, (blk, 2 * blk), 0)
    ki = lax.broadcasted_iota(jnp.int32, (blk, 2 * blk), 1)
    band = (ki >= qi) & (ki <= qi + DSA_SPAN)
    causal = (lax.broadcasted_iota(jnp.int32, (blk, blk), 1)
              <= lax.broadcasted_iota(jnp.int32, (blk, blk), 0))

    def bf(x, scale=None):
        return (x if scale is None else x * scale).astype(BF16)

    def attend_block(q, k, v, start, has_prev):
        if has_prev:
            return _dsa_attend(bf(q(start, blk), QK_SCALE), bf(k(start - blk, 2 * blk)),
                               bf(v(start - blk, 2 * blk)), band, head0)
        return _dsa_attend(bf(q(start, blk), QK_SCALE), bf(k(start, blk)), bf(v(start, blk)),
                           causal, head0)

    nat = lambda ref: (lambda start, n: ref[0, start:start + n, :])
    seg = lambda ref: (lambda start, n: ref[start:start + n, :])
    state = (acc_ref, m_ref, l_ref)
    state4 = (acc4_ref, m4_ref, l4_ref)

    for n in range(n_blocks):
        new = attend_block(nat(q_ref), nat(k_ref), nat(v_ref), n * blk, n > 0)
        for ref, val in zip(state, new):
            ref[n * blk:(n + 1) * blk, :] = val

    d_mid = DSA_DILATIONS[1]
    seg_len = seq // d_mid
    for rho in range(d_mid):
        dst = slice(rho * seg_len, (rho + 1) * seg_len)
        for src, dst_ref in ((q_ref, q4_ref), (k_ref, k4_ref), (v_ref, v4_ref)):
            dst_ref[dst, :] = src[0, pl.ds(rho, seg_len, stride=d_mid), :]
        for src, dst_ref in zip(state, state4):
            dst_ref[dst, :] = src[pl.ds(rho, seg_len, stride=d_mid), :]

    for rho in range(d_mid):
        for n in range(seg_len // blk):
            start = rho * seg_len + n * blk
            new = attend_block(seg(q4_ref), seg(k4_ref), seg(v4_ref), start, n > 0)
            old = tuple(ref[start:start + blk, :] for ref in state4)
            for ref, val in zip(state4, _softmax_merge(*old, *new)):
                ref[start:start + blk, :] = val

    inner = DSA_DILATIONS[2] // d_mid
    for rho in range(d_mid):
        for c in range(inner):
            rs = pl.ds(rho * seg_len + c, blk, stride=inner)
            new = _dsa_attend(bf(q4_ref[rs, :], QK_SCALE), bf(k4_ref[rs, :]), bf(v4_ref[rs, :]),
                              causal, head0)
            old = tuple(ref[rs, :] for ref in state4)
            num, _, den = _softmax_merge(*old, *new)
            acc4_ref[rs, :] = num / den

    for rho in range(d_mid):
        o_ref[0, pl.ds(rho, seg_len, stride=d_mid), :] = acc4_ref[rho * seg_len:(rho + 1) * seg_len, :]


def _dsa(qkv):
    B, S, _ = qkv.shape
    assert S == DSA_DILATIONS[2] * DSA_BLOCK, S
    blockspec = lambda off: pl.BlockSpec((1, S, LANES), lambda b, p: (b, 0, off + p))
    return pl.pallas_call(
        functools.partial(_dsa_kernel, seq=S),
        grid=(B, HEAD_PAIRS),
        in_specs=[blockspec(0), blockspec(HEAD_PAIRS), blockspec(2 * HEAD_PAIRS)],
        out_specs=blockspec(0),
        out_shape=jax.ShapeDtypeStruct((B, S, ATT_WIDTH), F32),
        scratch_shapes=[pltpu.VMEM((S, LANES), F32)] * 9,
        compiler_params=_params("parallel", "parallel"),
        name="dsa",
    )(qkv, qkv, qkv)


def _fox_kernel(q_ref, k_ref, vt_ref, ft_ref, bf_ref, o_ref, qa_ref, ka_ref, vx_ref, *, seq, tq):
    x = ft_ref[0] + bf_ref[...]
    logf = jnp.minimum(x, 0.0) - jnp.log1p(jnp.exp(-jnp.abs(x)))
    lane_t = lax.broadcasted_iota(jnp.int32, logf.shape, 1)
    cum = logf
    shift = 1
    while shift < seq:
        cum = cum + jnp.where(lane_t >= shift, pltpu.roll(cum, shift, 1), 0.0)
        shift *= 2

    stacked = jnp.concatenate([jnp.broadcast_to(cum[0:1, :], (HEAD_DIM, seq)),
                               jnp.broadcast_to(cum[1:2, :], (HEAD_DIM, seq))], axis=0)
    cc = stacked.T
    hi = cc.astype(BF16).astype(F32)
    r1 = cc - hi
    lo = r1.astype(BF16).astype(F32)
    lo2 = (r1 - lo).astype(BF16).astype(F32)
    ln = lax.broadcasted_iota(jnp.int32, (seq, LANES), 1)
    one = jnp.ones((seq, LANES), F32)
    zero = jnp.zeros((seq, LANES), F32)
    qv = q_ref[0].astype(F32) * QK_SCALE
    kv = k_ref[0]
    for h in range(2):
        b0 = HEAD_DIM * h
        in_head = (ln >= b0) & (ln < b0 + HEAD_DIM)
        first3 = (ln >= b0) & (ln < b0 + 3)
        next3 = (ln >= b0 + 3) & (ln < b0 + 6)
        aq = jnp.where(ln == b0, hi, jnp.where(ln == b0 + 1, lo, jnp.where(ln == b0 + 2, lo2,
                       jnp.where(next3, one, zero))))
        ak = jnp.where(ln == b0 + 3, -hi, jnp.where(ln == b0 + 4, -lo, jnp.where(ln == b0 + 5, -lo2,
                       jnp.where(first3, one, zero))))
        qa_ref[h, :, :LANES] = jnp.where(in_head, qv, zero).astype(BF16)
        qa_ref[h, :, LANES:] = aq.astype(BF16)
        ka_ref[h, :, :LANES] = kv
        ka_ref[h, :, LANES:] = ak.astype(BF16)

    vx_ref[:LANES, :] = vt_ref[0]
    vx_ref[LANES:, :] = jnp.ones((BF16_SUBLANES, seq), BF16)

    key_le_query = (lax.broadcasted_iota(jnp.int32, (tq, tq), 0)
                    <= lax.broadcasted_iota(jnp.int32, (tq, tq), 1))

    for iq in range(seq // tq):
        q0 = iq * tq
        outs = []
        for h in range(2):
            qa = qa_ref[h, q0:q0 + tq, :]
            s_dg = lax.dot_general(ka_ref[h, q0:q0 + tq, :], qa, NT_DIMS, preferred_element_type=F32)
            s_dg = jnp.where(key_le_query, s_dg, NEG)
            m = jnp.max(s_dg, axis=0, keepdims=True)
            if iq > 0:
                s_off = lax.dot_general(ka_ref[h, :q0, :], qa, NT_DIMS, preferred_element_type=F32)
                m = jnp.maximum(m, jnp.max(s_off, axis=0, keepdims=True))
            acc = jnp.dot(vx_ref[:, q0:q0 + tq], jnp.exp(s_dg - m).astype(BF16),
                          preferred_element_type=F32)
            if iq > 0:
                acc += jnp.dot(vx_ref[:, :q0], jnp.exp(s_off - m).astype(BF16),
                               preferred_element_type=F32)
            outs.append(acc[HEAD_DIM * h:HEAD_DIM * (h + 1), :] / acc[LANES:LANES + 1, :])
        o_ref[0, q0:q0 + tq, :] = jnp.concatenate(outs, axis=0).T


def _fox(qk, vt, ft, bf, *, tq):
    B, S, _ = qk.shape
    tok = lambda off: pl.BlockSpec((1, S, LANES), lambda b, p: (b, 0, off + p))
    feat = lambda rows: pl.BlockSpec((1, rows, S), lambda b, p: (b, p, 0))
    return pl.pallas_call(
        functools.partial(_fox_kernel, seq=S, tq=tq),
        grid=(B, HEAD_PAIRS),
        in_specs=[tok(0), tok(HEAD_PAIRS), feat(LANES), feat(8),
                  pl.BlockSpec((8, S), lambda b, p: (p, 0))],
        out_specs=tok(0),
        out_shape=jax.ShapeDtypeStruct((B, S, ATT_WIDTH), F32),
        scratch_shapes=[pltpu.VMEM((2, S, 2 * LANES), BF16), pltpu.VMEM((2, S, 2 * LANES), BF16),
                        pltpu.VMEM((LANES + BF16_SUBLANES, S), BF16)],
        compiler_params=_params("parallel", "parallel"),
        name="fox",
    )(qk, qk, vt, ft, bf)


MXU_WIDTH = 256


def _hidden_chunks(d_ff, target):
    assert d_ff % MXU_WIDTH == 0, d_ff
    tiles = d_ff // MXU_WIDTH
    n = -(-d_ff // target)
    sizes = [(tiles // n + (i < tiles % n)) * MXU_WIDTH for i in range(n)]
    starts = [sum(sizes[:i]) for i in range(n)]
    return [slice(a, a + w) for a, w in zip(starts, sizes)]


def _mixffn_kernel(x_ref, ys_ref, yd_ref, yf_ref, gs_ref, gd_ref, gf_ref, ws_ref, wd_ref, wf_ref,
                   gp_ref, g1_ref, wg_ref, wu_ref, wdn_ref, g2_ref, o_ref, *, f_target):
    o = jnp.dot(_rms_scale(ys_ref[...], gs_ref[...]).astype(BF16), ws_ref[...],
                preferred_element_type=F32)
    o += jnp.dot(_rms_scale(yd_ref[0], gd_ref[...]).astype(BF16), wd_ref[...],
                 preferred_element_type=F32)
    o += jnp.dot(_rms_scale(yf_ref[0], gf_ref[...]).astype(BF16), wf_ref[...],
                 preferred_element_type=F32)
    x = x_ref[0] + _rms_scale(o, gp_ref[...])

    h = _rms_scale(x, g1_ref[...]).astype(BF16)
    acc = jnp.zeros(x.shape, F32)
    for cols in _hidden_chunks(wg_ref.shape[1], f_target):
        gate = jnp.dot(h, wg_ref[:, cols], preferred_element_type=F32)
        up = jnp.dot(h, wu_ref[:, cols], preferred_element_type=F32)
        act = (gate * jax.nn.sigmoid(gate) * up).astype(BF16)
        acc += jnp.dot(act, wdn_ref[cols, :], preferred_element_type=F32)
    o_ref[0] = x + _rms_scale(acc, g2_ref[...])


def _mixffn(x, ys_sb, yd, yf, gs, gd, gf, ws, wd, wf, gp, g1, wg, wu, wdn, g2, *, tm, f_target):
    B, S, D = x.shape
    const = lambda b, j: (0, 0)
    tok = lambda w: pl.BlockSpec((1, tm, w), lambda b, j: (b, j, 0))
    full = lambda a: pl.BlockSpec(a.shape, const)
    weights = (gs, gd, gf, ws, wd, wf, gp, g1, wg, wu, wdn, g2)
    return pl.pallas_call(
        functools.partial(_mixffn_kernel, f_target=f_target),
        grid=(B, S // tm),
        in_specs=[tok(D), pl.BlockSpec((tm, SSM_WIDTH), lambda b, j: (j, b)), tok(ATT_WIDTH),
                  tok(ATT_WIDTH)] + [full(a) for a in weights],
        out_specs=tok(D),
        out_shape=jax.ShapeDtypeStruct((B, S, D), F32),
        compiler_params=_params("parallel", "parallel"),
        name="mixffn",
    )(x, ys_sb, yd, yf, *weights)


def _s5_matrices(log_dt, a_re, a_im, b_re, b_im, c_re, c_im):
    G, P = a_re.shape
    C = b_re.shape[-1]
    dt = jnp.exp(log_dt)[:, None]
    decay = jnp.exp(a_re * dt)
    abar_re = decay * jnp.cos(a_im * dt)
    abar_im = decay * jnp.sin(a_im * dt)
    mag2 = a_re * a_re + a_im * a_im
    coef_re = ((abar_re - 1.0) * a_re + abar_im * a_im) / mag2
    coef_im = (abar_im * a_re - (abar_re - 1.0) * a_im) / mag2
    bbar_re = coef_re[..., None] * b_re - coef_im[..., None] * b_im
    bbar_im = coef_re[..., None] * b_im + coef_im[..., None] * b_re
    eye = jnp.eye(G, dtype=F32)
    bt_re = jnp.einsum('gpc,gh->gchp', bbar_re, eye).reshape(G * C, G * P)
    bt_im = jnp.einsum('gpc,gh->gchp', bbar_im, eye).reshape(G * C, G * P)
    bt = jnp.concatenate([bt_re, bt_im], axis=1).astype(BF16)
    ct_re = jnp.einsum('gcp,gh->gphc', c_re, eye).reshape(G * P, G * C)
    ct_im = jnp.einsum('gcp,gh->gphc', -c_im, eye).reshape(G * P, G * C)
    ct = jnp.concatenate([ct_re, ct_im], axis=0).astype(BF16)
    return bt, abar_re.reshape(1, G * P), abar_im.reshape(1, G * P), ct


def _per_pair_rows(v):
    out = jnp.zeros((HEAD_PAIRS, 8) + v.shape[1:], v.dtype)
    out = out.at[:, :2].set(v.reshape((HEAD_PAIRS, 2) + v.shape[1:]))
    return out.reshape((HEAD_PAIRS * 8,) + v.shape[1:])


def kernel(x, g_pre_mix, w_in, b_f, ssm_log_dt, ssm_a_re, ssm_a_im, ssm_b_re, ssm_b_im, ssm_c_re, ssm_c_im, ssm_d, w_glu, b_glu, g_mix, w_out, g_post_mix, g_pre_ffn, w_gate_up, w_down, g_post_ffn):
    B, S, D = x.shape
    depth = w_in.shape[0]
    d_ff = w_down.shape[1]
    tm = 512
    o1 = SSM_WIDTH
    o2 = o1 + 3 * ATT_WIDTH
    o3 = o2 + 2 * ATT_WIDTH
    o4 = o3 + ATT_WIDTH
    row = lambda v: v.reshape(1, -1)

    for l in range(depth):
        w = w_in[l]
        u_sb, qkv_dsa, qk_fox, vt_fox, ft = _premix(
            x, row(g_pre_mix[l]), w[:, :o1].astype(BF16), w[:, o1:o2].astype(BF16),
            w[:, o2:o3].astype(BF16), w[:, o3:o4].T.astype(BF16),
            _per_pair_rows(w[:, o4:].T).astype(BF16), tm=tm)

        bt, a_re, a_im, ct = _s5_matrices(ssm_log_dt[l], ssm_a_re[l], ssm_a_im[l], ssm_b_re[l],
                                          ssm_b_im[l], ssm_c_re[l], ssm_c_im[l])
        y_ssm = _s5(u_sb.reshape(S * B, SSM_WIDTH), bt, a_re, a_im, ct, row(ssm_d[l]),
                    w_glu[l].astype(BF16), row(b_glu[l]), batch=B, steps=32)

        y_dsa = _dsa(qkv_dsa)

        bf = jnp.broadcast_to(_per_pair_rows(b_f[l][:, None]), (8 * HEAD_PAIRS, S))
        y_fox = _fox(qk_fox, vt_fox, ft, bf, tq=256)

        gm = g_mix[l]
        wo = w_out[l].astype(BF16)
        wgu = w_gate_up[l].astype(BF16)
        x = _mixffn(x, y_ssm.reshape(S, B * SSM_WIDTH), y_dsa, y_fox,
                    row(gm[:o1]), row(gm[o1:o1 + ATT_WIDTH]), row(gm[o1 + ATT_WIDTH:]),
                    wo[:o1], wo[o1:o1 + ATT_WIDTH], wo[o1 + ATT_WIDTH:], row(g_post_mix[l]),
                    row(g_pre_ffn[l]), wgu[:, :d_ff], wgu[:, d_ff:], w_down[l].astype(BF16),
                    row(g_post_ffn[l]), tm=tm, f_target=1536)
    return x
```

```python
import functools

import jax
import jax.numpy as jnp
from jax import lax
from jax.experimental import pallas as pl
from jax.experimental.pallas import tpu as pltpu

F32 = jnp.float32
BF16 = jnp.bfloat16

HEAD_DIM = 64
SSM_WIDTH = 256
ATT_WIDTH = 384
N_HEADS = 6
HEAD_PAIRS = N_HEADS // 2
DSA_SPAN = 128
DSA_BLOCK = 128
DSA_DILATIONS = (1, 4, 16)
RMS_EPS = 1e-6
NEG = -1e30
LOG2E = 1.4426950408889634
Q_SCALE = HEAD_DIM ** -0.5 * LOG2E
FOX_LOOKAHEAD = 5

LANES = 128
BF16_SUBLANES = 16
VMEM_LIMIT = 56 * 1024 * 1024

NT_DIMS = (((1,), (1,)), ((), ()))


def _rms_scale(x, g):
    ms = jnp.mean(x * x, axis=-1, keepdims=True)
    return x * lax.rsqrt(ms + RMS_EPS) * g


def _params(*sem):
    return pltpu.CompilerParams(dimension_semantics=sem, vmem_limit_bytes=VMEM_LIMIT)


def _premix_kernel(x_ref, g_ref, wu_ref, wd_ref, wk_ref, wqt_ref, wvt_ref, wft_ref,
                   u_ref, qd_ref, k_ref, qt_ref, vt_ref, ft_ref):
    h = _rms_scale(x_ref[0], g_ref[...]).astype(BF16)
    u_ref[...] = jnp.dot(h, wu_ref[...], preferred_element_type=F32)
    qd_ref[0] = jnp.dot(h, wd_ref[...], preferred_element_type=F32)
    k_ref[0] = jnp.dot(h, wk_ref[...], preferred_element_type=F32).astype(BF16)
    nt = lambda w_ref: lax.dot_general(w_ref[...], h, NT_DIMS, preferred_element_type=F32)
    qt_ref[0] = (nt(wqt_ref) * Q_SCALE).astype(BF16)
    vt_ref[0] = nt(wvt_ref).astype(BF16)
    ft_ref[0] = nt(wft_ref)


def _premix(x, g, wu, wd, wk, wqt, wvt, wft, *, tm):
    B, S, D = x.shape
    grid = (B, S // tm)
    const = lambda b, j: (0, 0)
    full = lambda a: pl.BlockSpec(a.shape, const)
    tok = lambda w: pl.BlockSpec((1, tm, w), lambda b, j: (b, j, 0))
    feat = lambda r: pl.BlockSpec((1, r, tm), lambda b, j: (b, 0, j))
    return pl.pallas_call(
        _premix_kernel,
        grid=grid,
        in_specs=[tok(D), full(g), full(wu), full(wd), full(wk), full(wqt), full(wvt), full(wft)],
        out_specs=[
            pl.BlockSpec((tm, SSM_WIDTH), lambda b, j: (j, b)),
            tok(3 * ATT_WIDTH), tok(ATT_WIDTH), feat(ATT_WIDTH), feat(ATT_WIDTH),
            feat(8 * HEAD_PAIRS),
        ],
        out_shape=[
            jax.ShapeDtypeStruct((S, B * SSM_WIDTH), F32),
            jax.ShapeDtypeStruct((B, S, 3 * ATT_WIDTH), F32),
            jax.ShapeDtypeStruct((B, S, ATT_WIDTH), BF16),
            jax.ShapeDtypeStruct((B, ATT_WIDTH, S), BF16),
            jax.ShapeDtypeStruct((B, ATT_WIDTH, S), BF16),
            jax.ShapeDtypeStruct((B, 8 * HEAD_PAIRS, S), F32),
        ],
        compiler_params=_params("parallel", "parallel"),
        name="premix",
    )(x, g, wu, wd, wk, wqt, wvt, wft)


def _s5_kernel(u_ref, bt_ref, are_ref, aim_ref, ct_ref, d_ref, wg_ref, bg_ref, y_ref,
               xs_ref, h_ref, *, batch, steps, col_tile):
    n_state = are_ref.shape[1]

    @pl.when(pl.program_id(0) == 0)
    def _():
        h_ref[...] = jnp.zeros_like(h_ref)

    u = u_ref[...]
    xs_ref[...] = jnp.dot(u.astype(BF16), bt_ref[...], preferred_element_type=F32)

    for c in range(n_state // col_tile):
        re = slice(c * col_tile, (c + 1) * col_tile)
        im = slice(n_state + c * col_tile, n_state + (c + 1) * col_tile)
        ar = are_ref[:, re]
        ai = aim_ref[:, re]

        def step(s, carry, re=re, im=im, ar=ar, ai=ai):
            hr, hi = carry
            rows = pl.ds(pl.multiple_of(s * batch, batch), batch)
            nr = ar * hr - ai * hi + xs_ref[rows, re]
            ni = ar * hi + ai * hr + xs_ref[rows, im]
            xs_ref[rows, re] = nr
            xs_ref[rows, im] = ni
            return nr, ni

        hr, hi = lax.fori_loop(0, steps, step, (h_ref[:, re], h_ref[:, im]), unroll=2)
        h_ref[:, re] = hr
        h_ref[:, im] = hi

    y = jnp.dot(xs_ref[...].astype(BF16), ct_ref[...], preferred_element_type=F32)
    y = y + d_ref[...] * u
    g = jax.nn.gelu(y)
    z = jnp.dot(g.astype(BF16), wg_ref[...], preferred_element_type=F32) + bg_ref[...]
    y_ref[...] = g * jax.nn.sigmoid(z)


def _s5(u_sb, bt, a_re, a_im, ct, d, wg, bg, *, batch, steps):
    rows_total, width = u_sb.shape
    rows = steps * batch
    n_cols = bt.shape[1]
    const = lambda i: (0, 0)
    kern = functools.partial(_s5_kernel, batch=batch, steps=steps, col_tile=2 * LANES)
    return pl.pallas_call(
        kern,
        grid=(rows_total // rows,),
        in_specs=[
            pl.BlockSpec((rows, width), lambda i: (i, 0)),
            pl.BlockSpec(bt.shape, const),
            pl.BlockSpec(a_re.shape, const),
            pl.BlockSpec(a_im.shape, const),
            pl.BlockSpec(ct.shape, const),
            pl.BlockSpec(d.shape, const),
            pl.BlockSpec(wg.shape, const),
            pl.BlockSpec(bg.shape, const),
        ],
        out_specs=pl.BlockSpec((rows, width), lambda i: (i, 0)),
        out_shape=jax.ShapeDtypeStruct((rows_total, width), F32),
        scratch_shapes=[
            pltpu.VMEM((rows, n_cols), F32),
            pltpu.VMEM((batch, n_cols), F32),
        ],
        compiler_params=_params("arbitrary"),
        name="s5",
    )(u_sb, bt, a_re, a_im, ct, d, wg, bg)


def _dsa_attend(qb, kw, vw, valid, head0):
    vx = jnp.concatenate([vw, jnp.ones_like(vw)], axis=1)
    parts = []
    for h in range(2):
        qh = jnp.where(head0 if h == 0 else jnp.logical_not(head0), qb, jnp.zeros_like(qb))
        s = lax.dot_general(qh, kw, NT_DIMS, preferred_element_type=F32)
        s = jnp.where(valid, s, NEG)
        m = jnp.max(s, axis=-1, keepdims=True)
        p = jnp.exp2(s - m).astype(BF16)
        r = jnp.dot(p, vx, preferred_element_type=F32)
        parts.append((r[:, :LANES], jnp.broadcast_to(m, (qb.shape[0], LANES)), r[:, LANES:]))
    return tuple(jnp.where(head0, a, b) for a, b in zip(*parts))


def _softmax_merge(n1, m1, l1, n2, m2, l2):
    m = jnp.maximum(m1, m2)
    e1 = jnp.exp2(m1 - m)
    e2 = jnp.exp2(m2 - m)
    return n1 * e1 + n2 * e2, m, l1 * e1 + l2 * e2


def _dsa_kernel(q_ref, k_ref, v_ref, o_ref, acc_ref, m_ref, l_ref,
                q4_ref, k4_ref, v4_ref, acc4_ref, m4_ref, l4_ref, *, seq):
    blk = DSA_BLOCK
    n_blocks = seq // blk
    head0 = lax.broadcasted_iota(jnp.int32, (blk, LANES), 1) < HEAD_DIM
    qi = lax.broadcasted_iota(jnp.int32, (blk, 2 * blk), 0)
    ki = lax.broadcasted_iota(jnp.int32, (blk, 2 * blk), 1)
    band = (ki >= qi) & (ki <= qi + DSA_SPAN)
    causal = (lax.broadcasted_iota(jnp.int32, (blk, blk), 1)
              <= lax.broadcasted_iota(jnp.int32, (blk, blk), 0))

    def bf(x, scale=None):
        return (x if scale is None else x * scale).astype(BF16)

    def attend_block(q, k, v, start, has_prev):
        if has_prev:
            return _dsa_attend(bf(q(start, blk), Q_SCALE), bf(k(start - blk, 2 * blk)),
                               bf(v(start - blk, 2 * blk)), band, head0)
        return _dsa_attend(bf(q(start, blk), Q_SCALE), bf(k(start, blk)), bf(v(start, blk)),
                           causal, head0)

    nat = lambda ref: (lambda start, n: ref[0, start:start + n, :])
    seg = lambda ref: (lambda start, n: ref[start:start + n, :])
    state = (acc_ref, m_ref, l_ref)
    state4 = (acc4_ref, m4_ref, l4_ref)

    for n in range(n_blocks):
        new = attend_block(nat(q_ref), nat(k_ref), nat(v_ref), n * blk, n > 0)
        for ref, val in zip(state, new):
            ref[n * blk:(n + 1) * blk, :] = val

    d_mid = DSA_DILATIONS[1]
    seg_len = seq // d_mid
    for rho in range(d_mid):
        dst = slice(rho * seg_len, (rho + 1) * seg_len)
        for src, dst_ref in ((q_ref, q4_ref), (k_ref, k4_ref), (v_ref, v4_ref)):
            dst_ref[dst, :] = src[0, pl.ds(rho, seg_len, stride=d_mid), :]
        for src, dst_ref in zip(state, state4):
            dst_ref[dst, :] = src[pl.ds(rho, seg_len, stride=d_mid), :]

    for rho in range(d_mid):
        for n in range(seg_len // blk):
            start = rho * seg_len + n * blk
            new = attend_block(seg(q4_ref), seg(k4_ref), seg(v4_ref), start, n > 0)
            old = tuple(ref[start:start + blk, :] for ref in state4)
            for ref, val in zip(state4, _softmax_merge(*old, *new)):
                ref[start:start + blk, :] = val

    inner = DSA_DILATIONS[2] // d_mid
    for rho in range(d_mid):
        for c in range(inner):
            rs = pl.ds(rho * seg_len + c, blk, stride=inner)
            new = _dsa_attend(bf(q4_ref[rs, :], Q_SCALE), bf(k4_ref[rs, :]), bf(v4_ref[rs, :]),
                              causal, head0)
            old = tuple(ref[rs, :] for ref in state4)
            num, _, den = _softmax_merge(*old, *new)
            acc4_ref[rs, :] = num / den

    for rho in range(d_mid):
        o_ref[0, pl.ds(rho, seg_len, stride=d_mid), :] = acc4_ref[rho * seg_len:(rho + 1) * seg_len, :]


def _dsa(qkv):
    B, S, _ = qkv.shape
    assert S == DSA_DILATIONS[2] * DSA_BLOCK, S
    blockspec = lambda off: pl.BlockSpec((1, S, LANES), lambda b, p: (b, 0, off + p))
    return pl.pallas_call(
        functools.partial(_dsa_kernel, seq=S),
        grid=(B, HEAD_PAIRS),
        in_specs=[blockspec(0), blockspec(HEAD_PAIRS), blockspec(2 * HEAD_PAIRS)],
        out_specs=blockspec(0),
        out_shape=jax.ShapeDtypeStruct((B, S, ATT_WIDTH), F32),
        scratch_shapes=[pltpu.VMEM((S, LANES), F32)] * 9,
        compiler_params=_params("parallel", "parallel"),
        name="dsa",
    )(qkv, qkv, qkv)


def _fox_kernel(k_ref, qt_ref, vt_ref, ft_ref, bf_ref, o_ref, qa_ref, ka_ref, vx_ref, *, seq, tq):
    x = ft_ref[0] + bf_ref[...]
    logf = jnp.minimum(x, 0.0) - jnp.log1p(jnp.exp(-jnp.abs(x)))
    lane_t = lax.broadcasted_iota(jnp.int32, logf.shape, 1)
    cum = logf
    shift = 1
    while shift < seq:
        cum = cum + jnp.where(lane_t >= shift, pltpu.roll(cum, shift, 1), 0.0)
        shift *= 2

    cum = cum * LOG2E
    hi = cum.astype(BF16).astype(F32)
    r1 = cum - hi
    lo = r1.astype(BF16).astype(F32)
    lo2 = (r1 - lo).astype(BF16).astype(F32)

    sub = lax.broadcasted_iota(jnp.int32, (8, seq), 0)
    rows8 = lambda v, r: jnp.broadcast_to(v[r:r + 1, :], (8, seq))
    pad = jnp.zeros((HEAD_DIM - 8, seq), F32)
    q_tiles, k_tiles = [], []
    for r in range(2):
        h8, l8, l28 = rows8(hi, r), rows8(lo, r), rows8(lo2, r)
        q_tiles += [jnp.where(sub == 0, h8, jnp.where(sub == 1, l8, jnp.where(sub == 2, l28,
                              jnp.where(sub < 6, 1.0, 0.0)))), pad]
        k_tiles += [jnp.where(sub < 3, 1.0, jnp.where(sub == 3, -h8, jnp.where(sub == 4, -l8,
                              jnp.where(sub == 5, -l28, 0.0)))), pad]
    aq_t = jnp.concatenate(q_tiles, axis=0).astype(BF16)
    ak = jnp.concatenate(k_tiles, axis=0).T

    lane = lax.broadcasted_iota(jnp.int32, (1, LANES), 1)
    qt = qt_ref[0]
    kv = k_ref[0]
    zeros_t = jnp.zeros((HEAD_DIM, seq), BF16)
    for h in range(2):
        in_head = (lane >= HEAD_DIM * h) & (lane < HEAD_DIM * (h + 1))
        qa_ref[h, :LANES, :] = (jnp.concatenate([qt[:HEAD_DIM], zeros_t], axis=0) if h == 0 else
                                jnp.concatenate([zeros_t, qt[HEAD_DIM:]], axis=0))
        qa_ref[h, LANES:, :] = aq_t
        ka_ref[h, :, :LANES] = kv
        ka_ref[h, :, LANES:] = (ak * in_head.astype(F32)).astype(BF16)

    vx_ref[:LANES, :] = vt_ref[0]
    vx_ref[LANES:, :] = jnp.ones((BF16_SUBLANES, seq), BF16)

    key_le_query = (lax.broadcasted_iota(jnp.int32, (tq, tq), 0)
                    <= lax.broadcasted_iota(jnp.int32, (tq, tq), 1))

    n_q = seq // tq
    steps = [(j, iq, h) for j in range(n_q) for iq in range(j, n_q) for h in range(2)]
    state = {}
    outs = {}

    def scores(j, iq, h):
        s = jnp.dot(ka_ref[h, j * tq:(j + 1) * tq, :], qa_ref[h, :, iq * tq:(iq + 1) * tq],
                    preferred_element_type=F32)
        return jnp.where(key_le_query, s, NEG) if j == iq else s

    def absorb(j, iq, h, s):
        m = jnp.max(s, axis=0, keepdims=True)
        if j > 0:
            m_old, acc_old = state[iq, h]
            m = jnp.maximum(m_old, m)
        acc = jnp.dot(vx_ref[:, j * tq:(j + 1) * tq], jnp.exp2(s - m).astype(BF16),
                      preferred_element_type=F32)
        if j > 0:
            acc = acc_old * jnp.exp2(m_old - m) + acc
        state[iq, h] = (m, acc)
        if j == iq:
            del state[iq, h]
            outs[iq, h] = acc[HEAD_DIM * h:HEAD_DIM * (h + 1), :] / acc[LANES:LANES + 1, :]
            if h == 1:
                o_ref[0, iq * tq:(iq + 1) * tq, :] = jnp.concatenate(
                    [outs.pop((iq, 0)), outs.pop((iq, 1))], axis=0).T

    pending = []
    for step in steps:
        pending.append((step, scores(*step)))
        if len(pending) > FOX_LOOKAHEAD:
            done, s = pending.pop(0)
            absorb(*done, s)
    for done, s in pending:
        absorb(*done, s)


def _fox(k, qt, vt, ft, bf, *, tq):
    B, S, _ = k.shape
    tok = pl.BlockSpec((1, S, LANES), lambda b, p: (b, 0, p))
    feat = lambda rows: pl.BlockSpec((1, rows, S), lambda b, p: (b, p, 0))
    return pl.pallas_call(
        functools.partial(_fox_kernel, seq=S, tq=tq),
        grid=(B, HEAD_PAIRS),
        in_specs=[tok, feat(LANES), feat(LANES), feat(8),
                  pl.BlockSpec((8, S), lambda b, p: (p, 0))],
        out_specs=tok,
        out_shape=jax.ShapeDtypeStruct((B, S, ATT_WIDTH), F32),
        scratch_shapes=[pltpu.VMEM((2, 2 * LANES, S), BF16),
                        pltpu.VMEM((2, S, 2 * LANES), BF16),
                        pltpu.VMEM((LANES + BF16_SUBLANES, S), BF16)],
        compiler_params=_params("parallel", "parallel"),
        name="fox",
    )(k, qt, vt, ft, bf)


MXU_WIDTH = 256


def _hidden_chunks(d_ff, target):
    assert d_ff % MXU_WIDTH == 0, d_ff
    tiles = d_ff // MXU_WIDTH
    n = -(-d_ff // target)
    sizes = [(tiles // n + (i < tiles % n)) * MXU_WIDTH for i in range(n)]
    starts = [sum(sizes[:i]) for i in range(n)]
    return [slice(a, a + w) for a, w in zip(starts, sizes)]


def _mixffn_kernel(x_ref, ys_ref, yd_ref, yf_ref, gs_ref, gd_ref, gf_ref, wo_ref,
                   gp_ref, g1_ref, wg_ref, wu_ref, wdn_ref, g2_ref, o_ref, *, f_target):
    mixed = jnp.concatenate([_rms_scale(ys_ref[...], gs_ref[...]).astype(BF16),
                             _rms_scale(yd_ref[0], gd_ref[...]).astype(BF16),
                             _rms_scale(yf_ref[0], gf_ref[...]).astype(BF16)], axis=1)
    o = jnp.dot(mixed, wo_ref[...], preferred_element_type=F32)
    x = x_ref[0] + _rms_scale(o, gp_ref[...])

    h = _rms_scale(x, g1_ref[...]).astype(BF16)
    acc = jnp.zeros(x.shape, F32)
    for cols in _hidden_chunks(wg_ref.shape[1], f_target):
        gate = jnp.dot(h, wg_ref[:, cols], preferred_element_type=F32)
        up = jnp.dot(h, wu_ref[:, cols], preferred_element_type=F32)
        act = (gate * jax.nn.sigmoid(gate) * up).astype(BF16)
        acc += jnp.dot(act, wdn_ref[cols, :], preferred_element_type=F32)
    o_ref[0] = x + _rms_scale(acc, g2_ref[...])


def _mixffn(x, ys_sb, yd, yf, gs, gd, gf, wo, gp, g1, wg, wu, wdn, g2, *, tm, f_target):
    B, S, D = x.shape
    const = lambda b, j: (0, 0)
    tok = lambda w: pl.BlockSpec((1, tm, w), lambda b, j: (b, j, 0))
    full = lambda a: pl.BlockSpec(a.shape, const)
    weights = (gs, gd, gf, wo, gp, g1, wg, wu, wdn, g2)
    return pl.pallas_call(
        functools.partial(_mixffn_kernel, f_target=f_target),
        grid=(B, S // tm),
        in_specs=[tok(D), pl.BlockSpec((tm, SSM_WIDTH), lambda b, j: (j, b)), tok(ATT_WIDTH),
                  tok(ATT_WIDTH)] + [full(a) for a in weights],
        out_specs=tok(D),
        out_shape=jax.ShapeDtypeStruct((B, S, D), F32),
        compiler_params=_params("parallel", "parallel"),
        name="mixffn",
    )(x, ys_sb, yd, yf, *weights)


def _s5_matrices(log_dt, a_re, a_im, b_re, b_im, c_re, c_im):
    G, P = a_re.shape
    C = b_re.shape[-1]
    dt = jnp.exp(log_dt)[:, None]
    decay = jnp.exp(a_re * dt)
    abar_re = decay * jnp.cos(a_im * dt)
    abar_im = decay * jnp.sin(a_im * dt)
    mag2 = a_re * a_re + a_im * a_im
    coef_re = ((abar_re - 1.0) * a_re + abar_im * a_im) / mag2
    coef_im = (abar_im * a_re - (abar_re - 1.0) * a_im) / mag2
    bbar_re = coef_re[..., None] * b_re - coef_im[..., None] * b_im
    bbar_im = coef_re[..., None] * b_im + coef_im[..., None] * b_re
    eye = jnp.eye(G, dtype=F32)
    bt_re = jnp.einsum('gpc,gh->gchp', bbar_re, eye).reshape(G * C, G * P)
    bt_im = jnp.einsum('gpc,gh->gchp', bbar_im, eye).reshape(G * C, G * P)
    bt = jnp.concatenate([bt_re, bt_im], axis=1).astype(BF16)
    ct_re = jnp.einsum('gcp,gh->gphc', c_re, eye).reshape(G * P, G * C)
    ct_im = jnp.einsum('gcp,gh->gphc', -c_im, eye).reshape(G * P, G * C)
    ct = jnp.concatenate([ct_re, ct_im], axis=0).astype(BF16)
    return bt, abar_re.reshape(1, G * P), abar_im.reshape(1, G * P), ct


def _per_pair_rows(v):
    out = jnp.zeros((HEAD_PAIRS, 8) + v.shape[1:], v.dtype)
    out = out.at[:, :2].set(v.reshape((HEAD_PAIRS, 2) + v.shape[1:]))
    return out.reshape((HEAD_PAIRS * 8,) + v.shape[1:])


def kernel(x, g_pre_mix, w_in, b_f, ssm_log_dt, ssm_a_re, ssm_a_im, ssm_b_re, ssm_b_im, ssm_c_re, ssm_c_im, ssm_d, w_glu, b_glu, g_mix, w_out, g_post_mix, g_pre_ffn, w_gate_up, w_down, g_post_ffn):
    B, S, D = x.shape
    depth = w_in.shape[0]
    d_ff = w_down.shape[1]
    tm = 512
    o1 = SSM_WIDTH
    o2 = o1 + 3 * ATT_WIDTH
    o3 = o2 + 2 * ATT_WIDTH
    o4 = o3 + ATT_WIDTH
    row = lambda v: v.reshape(1, -1)

    for l in range(depth):
        w = w_in[l]
        u_sb, qkv_dsa, k_fox, qt_fox, vt_fox, ft = _premix(
            x, row(g_pre_mix[l]), w[:, :o1].astype(BF16), w[:, o1:o2].astype(BF16),
            w[:, o2 + ATT_WIDTH:o3].astype(BF16), w[:, o2:o2 + ATT_WIDTH].T.astype(BF16),
            w[:, o3:o4].T.astype(BF16), _per_pair_rows(w[:, o4:].T).astype(BF16), tm=tm)

        bt, a_re, a_im, ct = _s5_matrices(ssm_log_dt[l], ssm_a_re[l], ssm_a_im[l], ssm_b_re[l],
                                          ssm_b_im[l], ssm_c_re[l], ssm_c_im[l])
        y_ssm = _s5(u_sb.reshape(S * B, SSM_WIDTH), bt, a_re, a_im, ct, row(ssm_d[l]),
                    w_glu[l].astype(BF16), row(b_glu[l]), batch=B, steps=32)

        y_dsa = _dsa(qkv_dsa)

        bf = jnp.broadcast_to(_per_pair_rows(b_f[l][:, None]), (8 * HEAD_PAIRS, S))
        y_fox = _fox(k_fox, qt_fox, vt_fox, ft, bf, tq=256)

        gm = g_mix[l]
        wo = w_out[l].astype(BF16)
        wgu = w_gate_up[l].astype(BF16)
        x = _mixffn(x, y_ssm.reshape(S, B * SSM_WIDTH), y_dsa, y_fox,
                    row(gm[:o1]), row(gm[o1:o1 + ATT_WIDTH]), row(gm[o1 + ATT_WIDTH:]),
                    wo, row(g_post_mix[l]),
                    row(g_pre_ffn[l]), wgu[:, :d_ff], wgu[:, d_ff:], w_down[l].astype(BF16),
                    row(g_post_ffn[l]), tm=tm, f_target=1536)
    return x
```

```python
import functools

import jax
import jax.numpy as jnp
from jax import lax
from jax.experimental import pallas as pl
from jax.experimental.pallas import tpu as pltpu

F32 = jnp.float32
BF16 = jnp.bfloat16

HEAD_DIM = 64
SSM_WIDTH = 256
ATT_WIDTH = 384
N_HEADS = 6
HEAD_PAIRS = N_HEADS // 2
DSA_SPAN = 128
DSA_BLOCK = 128
DSA_DILATIONS = (1, 4, 16)
RMS_EPS = 1e-6
NEG = -1e30
LOG2E = 1.4426950408889634
Q_SCALE = HEAD_DIM ** -0.5 * LOG2E
PRE_GROUPS = 2
DSA_LOOKAHEAD = 3
FOX_LOOKAHEAD = 5

LANES = 128
BF16_SUBLANES = 16
VMEM_LIMIT = 56 * 1024 * 1024

NT_DIMS = (((1,), (1,)), ((), ()))


def _rms_scale(x, g):
    ms = jnp.mean(x * x, axis=-1, keepdims=True)
    return x * lax.rsqrt(ms + RMS_EPS) * g


def _params(*sem):
    return pltpu.CompilerParams(dimension_semantics=sem, vmem_limit_bytes=VMEM_LIMIT)


def _premix_kernel(x_ref, g_ref, wu_ref, wdk_ref, wqt_ref, wvt_ref, wft_ref,
                   u_ref, qd_ref, k_ref, qt_ref, vt_ref, ft_ref):
    tm = x_ref.shape[1]
    groups = [slice(i * tm // PRE_GROUPS, (i + 1) * tm // PRE_GROUPS) for i in range(PRE_GROUPS)]
    hs = [_rms_scale(x_ref[0, r, :], g_ref[...]).astype(BF16) for r in groups]
    for r, h in zip(groups, hs):
        u_ref[r, :] = jnp.dot(h, wu_ref[...], preferred_element_type=F32)
        dk = jnp.dot(h, wdk_ref[...], preferred_element_type=F32)
        qd_ref[0, r, :] = dk[:, :3 * ATT_WIDTH]
        k_ref[0, r, :] = dk[:, 3 * ATT_WIDTH:].astype(BF16)
        nt = lambda w_ref: lax.dot_general(w_ref[...], h, NT_DIMS, preferred_element_type=F32)
        qt_ref[0, :, r] = (nt(wqt_ref) * Q_SCALE).astype(BF16)
        vt_ref[0, :, r] = nt(wvt_ref).astype(BF16)
        ft_ref[0, :, r] = nt(wft_ref)


def _premix(x, g, wu, wdk, wqt, wvt, wft, *, tm):
    B, S, D = x.shape
    grid = (B, S // tm)
    const = lambda b, j: (0, 0)
    full = lambda a: pl.BlockSpec(a.shape, const)
    tok = lambda w: pl.BlockSpec((1, tm, w), lambda b, j: (b, j, 0))
    feat = lambda r: pl.BlockSpec((1, r, tm), lambda b, j: (b, 0, j))
    return pl.pallas_call(
        _premix_kernel,
        grid=grid,
        in_specs=[tok(D), full(g), full(wu), full(wdk), full(wqt), full(wvt), full(wft)],
        out_specs=[
            pl.BlockSpec((tm, SSM_WIDTH), lambda b, j: (j, b)),
            tok(3 * ATT_WIDTH), tok(ATT_WIDTH), feat(ATT_WIDTH), feat(ATT_WIDTH),
            feat(8 * HEAD_PAIRS),
        ],
        out_shape=[
            jax.ShapeDtypeStruct((S, B * SSM_WIDTH), F32),
            jax.ShapeDtypeStruct((B, S, 3 * ATT_WIDTH), F32),
            jax.ShapeDtypeStruct((B, S, ATT_WIDTH), BF16),
            jax.ShapeDtypeStruct((B, ATT_WIDTH, S), BF16),
            jax.ShapeDtypeStruct((B, ATT_WIDTH, S), BF16),
            jax.ShapeDtypeStruct((B, 8 * HEAD_PAIRS, S), F32),
        ],
        compiler_params=_params("parallel", "parallel"),
        name="premix",
    )(x, g, wu, wdk, wqt, wvt, wft)


def _s5_kernel(u_ref, bt_ref, are_ref, aim_ref, ct_ref, d_ref, wg_ref, bg_ref, y_ref,
               xs_ref, h_ref, *, batch, steps, col_tile):
    n_state = are_ref.shape[1]

    @pl.when(pl.program_id(0) == 0)
    def _():
        h_ref[...] = jnp.zeros_like(h_ref)

    u = u_ref[...]
    xs_ref[...] = jnp.dot(u.astype(BF16), bt_ref[...], preferred_element_type=F32)

    for c in range(n_state // col_tile):
        re = slice(c * col_tile, (c + 1) * col_tile)
        im = slice(n_state + c * col_tile, n_state + (c + 1) * col_tile)
        ar = are_ref[:, re]
        ai = aim_ref[:, re]

        def step(s, carry, re=re, im=im, ar=ar, ai=ai):
            hr, hi = carry
            rows = pl.ds(pl.multiple_of(s * batch, batch), batch)
            nr = ar * hr - ai * hi + xs_ref[rows, re]
            ni = ar * hi + ai * hr + xs_ref[rows, im]
            xs_ref[rows, re] = nr
            xs_ref[rows, im] = ni
            return nr, ni

        hr, hi = lax.fori_loop(0, steps, step, (h_ref[:, re], h_ref[:, im]), unroll=2)
        h_ref[:, re] = hr
        h_ref[:, im] = hi

    y = jnp.dot(xs_ref[...].astype(BF16), ct_ref[...], preferred_element_type=F32)
    y = y + d_ref[...] * u
    g = jax.nn.gelu(y)
    z = jnp.dot(g.astype(BF16), wg_ref[...], preferred_element_type=F32) + bg_ref[...]
    y_ref[...] = g * jax.nn.sigmoid(z)


def _s5(u_sb, bt, a_re, a_im, ct, d, wg, bg, *, batch, steps):
    rows_total, width = u_sb.shape
    rows = steps * batch
    n_cols = bt.shape[1]
    const = lambda i: (0, 0)
    kern = functools.partial(_s5_kernel, batch=batch, steps=steps, col_tile=2 * LANES)
    return pl.pallas_call(
        kern,
        grid=(rows_total // rows,),
        in_specs=[
            pl.BlockSpec((rows, width), lambda i: (i, 0)),
            pl.BlockSpec(bt.shape, const),
            pl.BlockSpec(a_re.shape, const),
            pl.BlockSpec(a_im.shape, const),
            pl.BlockSpec(ct.shape, const),
            pl.BlockSpec(d.shape, const),
            pl.BlockSpec(wg.shape, const),
            pl.BlockSpec(bg.shape, const),
        ],
        out_specs=pl.BlockSpec((rows, width), lambda i: (i, 0)),
        out_shape=jax.ShapeDtypeStruct((rows_total, width), F32),
        scratch_shapes=[
            pltpu.VMEM((rows, n_cols), F32),
            pltpu.VMEM((batch, n_cols), F32),
        ],
        compiler_params=_params("arbitrary"),
        name="s5",
    )(u_sb, bt, a_re, a_im, ct, d, wg, bg)


def _dsa_scores(qb, kw, valid, head0):
    out = []
    for h in range(2):
        qh = jnp.where(head0 if h == 0 else jnp.logical_not(head0), qb, jnp.zeros_like(qb))
        s = lax.dot_general(qh, kw, NT_DIMS, preferred_element_type=F32)
        out.append(jnp.where(valid, s, NEG))
    return out


def _dsa_finish(scores, vw, head0):
    vx = jnp.concatenate([vw, jnp.ones_like(vw)], axis=1)
    parts = []
    for s in scores:
        m = jnp.max(s, axis=-1, keepdims=True)
        r = jnp.dot(jnp.exp2(s - m).astype(BF16), vx, preferred_element_type=F32)
        parts.append((r[:, :LANES], jnp.broadcast_to(m, (s.shape[0], LANES)), r[:, LANES:]))
    return tuple(jnp.where(head0, a, b) for a, b in zip(*parts))


def _softmax_merge(n1, m1, l1, n2, m2, l2):
    m = jnp.maximum(m1, m2)
    e1 = jnp.exp2(m1 - m)
    e2 = jnp.exp2(m2 - m)
    return n1 * e1 + n2 * e2, m, l1 * e1 + l2 * e2


def _dsa_kernel(q_ref, k_ref, v_ref, o_ref, acc_ref, m_ref, l_ref,
                q4_ref, k4_ref, v4_ref, acc4_ref, m4_ref, l4_ref, *, seq):
    blk = DSA_BLOCK
    n_blocks = seq // blk
    head0 = lax.broadcasted_iota(jnp.int32, (blk, LANES), 1) < HEAD_DIM
    qi = lax.broadcasted_iota(jnp.int32, (blk, 2 * blk), 0)
    ki = lax.broadcasted_iota(jnp.int32, (blk, 2 * blk), 1)
    band = (ki >= qi) & (ki <= qi + DSA_SPAN)
    causal = (lax.broadcasted_iota(jnp.int32, (blk, blk), 1)
              <= lax.broadcasted_iota(jnp.int32, (blk, blk), 0))

    def bf(x, scale=None):
        return (x if scale is None else x * scale).astype(BF16)

    pending = []

    def retire():
        scores, vw, sink = pending.pop(0)
        sink(_dsa_finish(scores, vw, head0))

    def submit(q, kw, vw, sink):
        valid = band if kw.shape[0] == 2 * blk else causal
        pending.append((_dsa_scores(bf(q, Q_SCALE), bf(kw), valid, head0), bf(vw), sink))
        if len(pending) > DSA_LOOKAHEAD:
            retire()

    def drain():
        while pending:
            retire()

    def window(ref_rows, start, has_prev):
        return ref_rows(start - blk, 2 * blk) if has_prev else ref_rows(start, blk)

    nat = lambda ref: (lambda start, n: ref[0, start:start + n, :])
    seg = lambda ref: (lambda start, n: ref[start:start + n, :])
    state = (acc_ref, m_ref, l_ref)
    state4 = (acc4_ref, m4_ref, l4_ref)

    d_mid = DSA_DILATIONS[1]
    seg_len = seq // d_mid
    for rho in range(d_mid):
        for src, dst_ref in ((q_ref, q4_ref), (k_ref, k4_ref), (v_ref, v4_ref)):
            dst_ref[rho * seg_len:(rho + 1) * seg_len, :] = src[0, pl.ds(rho, seg_len, stride=d_mid), :]

    def init_state(rows):
        def sink(new):
            for ref, val in zip(state, new):
                ref[rows, :] = val
        return sink

    for n in range(n_blocks):
        start = n * blk
        submit(nat(q_ref)(start, blk), window(nat(k_ref), start, n > 0),
               window(nat(v_ref), start, n > 0), init_state(slice(start, start + blk)))
    drain()

    for rho in range(d_mid):
        for src, dst_ref in zip(state, state4):
            dst_ref[rho * seg_len:(rho + 1) * seg_len, :] = src[pl.ds(rho, seg_len, stride=d_mid), :]

    def merge_state(rows):
        def sink(new):
            old = tuple(ref[rows, :] for ref in state4)
            for ref, val in zip(state4, _softmax_merge(*old, *new)):
                ref[rows, :] = val
        return sink

    for rho in range(d_mid):
        for n in range(seg_len // blk):
            start = rho * seg_len + n * blk
            submit(seg(q4_ref)(start, blk), window(seg(k4_ref), start, n > 0),
                   window(seg(v4_ref), start, n > 0), merge_state(slice(start, start + blk)))
    drain()

    def finalize(rows):
        def sink(new):
            old = tuple(ref[rows, :] for ref in state4)
            num, _, den = _softmax_merge(*old, *new)
            acc4_ref[rows, :] = num / den
        return sink

    inner = DSA_DILATIONS[2] // d_mid
    for rho in range(d_mid):
        for c in range(inner):
            rs = pl.ds(rho * seg_len + c, blk, stride=inner)
            submit(q4_ref[rs, :], k4_ref[rs, :], v4_ref[rs, :], finalize(rs))
    drain()

    for rho in range(d_mid):
        o_ref[0, pl.ds(rho, seg_len, stride=d_mid), :] = acc4_ref[rho * seg_len:(rho + 1) * seg_len, :]


def _dsa(qkv):
    B, S, _ = qkv.shape
    assert S == DSA_DILATIONS[2] * DSA_BLOCK, S
    blockspec = lambda off: pl.BlockSpec((1, S, LANES), lambda b, p: (b, 0, off + p))
    return pl.pallas_call(
        functools.partial(_dsa_kernel, seq=S),
        grid=(B, HEAD_PAIRS),
        in_specs=[blockspec(0), blockspec(HEAD_PAIRS), blockspec(2 * HEAD_PAIRS)],
        out_specs=blockspec(0),
        out_shape=jax.ShapeDtypeStruct((B, S, ATT_WIDTH), F32),
        scratch_shapes=[pltpu.VMEM((S, LANES), F32)] * 9,
        compiler_params=_params("parallel", "parallel"),
        name="dsa",
    )(qkv, qkv, qkv)


def _fox_kernel(k_ref, qt_ref, vt_ref, ft_ref, bf_ref, o_ref, qa_ref, ka_ref, vx_ref, *, seq, tq):
    x = ft_ref[0] + bf_ref[...]
    logf = jnp.minimum(x, 0.0) - jnp.log1p(jnp.exp(-jnp.abs(x)))
    lane_t = lax.broadcasted_iota(jnp.int32, logf.shape, 1)
    cum = logf
    shift = 1
    while shift < seq:
        cum = cum + jnp.where(lane_t >= shift, pltpu.roll(cum, shift, 1), 0.0)
        shift *= 2

    cum = cum * LOG2E
    hi = cum.astype(BF16).astype(F32)
    r1 = cum - hi
    lo = r1.astype(BF16).astype(F32)
    lo2 = (r1 - lo).astype(BF16).astype(F32)

    sub = lax.broadcasted_iota(jnp.int32, (8, seq), 0)
    rows8 = lambda v, r: jnp.broadcast_to(v[r:r + 1, :], (8, seq))
    pad = jnp.zeros((HEAD_DIM - 8, seq), F32)
    q_tiles, k_tiles = [], []
    for r in range(2):
        h8, l8, l28 = rows8(hi, r), rows8(lo, r), rows8(lo2, r)
        q_tiles += [jnp.where(sub == 0, h8, jnp.where(sub == 1, l8, jnp.where(sub == 2, l28,
                              jnp.where(sub < 6, 1.0, 0.0)))), pad]
        k_tiles += [jnp.where(sub < 3, 1.0, jnp.where(sub == 3, -h8, jnp.where(sub == 4, -l8,
                              jnp.where(sub == 5, -l28, 0.0)))), pad]
    aq_t = jnp.concatenate(q_tiles, axis=0).astype(BF16)
    ak = jnp.concatenate(k_tiles, axis=0).T

    lane = lax.broadcasted_iota(jnp.int32, (1, LANES), 1)
    qt = qt_ref[0]
    kv = k_ref[0]
    zeros_t = jnp.zeros((HEAD_DIM, seq), BF16)
    for h in range(2):
        in_head = (lane >= HEAD_DIM * h) & (lane < HEAD_DIM * (h + 1))
        qa_ref[h, :LANES, :] = (jnp.concatenate([qt[:HEAD_DIM], zeros_t], axis=0) if h == 0 else
                                jnp.concatenate([zeros_t, qt[HEAD_DIM:]], axis=0))
        qa_ref[h, LANES:, :] = aq_t
        ka_ref[h, :, :LANES] = kv
        ka_ref[h, :, LANES:] = (ak * in_head.astype(F32)).astype(BF16)

    vx_ref[:LANES, :] = vt_ref[0]
    vx_ref[LANES:, :] = jnp.ones((BF16_SUBLANES, seq), BF16)

    key_le_query = (lax.broadcasted_iota(jnp.int32, (tq, tq), 0)
                    <= lax.broadcasted_iota(jnp.int32, (tq, tq), 1))

    n_q = seq // tq
    steps = [(j, iq, h) for j in range(n_q) for iq in range(j, n_q) for h in range(2)]
    state = {}
    outs = {}

    def scores(j, iq, h):
        s = jnp.dot(ka_ref[h, j * tq:(j + 1) * tq, :], qa_ref[h, :, iq * tq:(iq + 1) * tq],
                    preferred_element_type=F32)
        return jnp.where(key_le_query, s, NEG) if j == iq else s

    def absorb(j, iq, h, s):
        m = jnp.max(s, axis=0, keepdims=True)
        if j > 0:
            m_old, acc_old = state[iq, h]
            m = jnp.maximum(m_old, m)
        acc = jnp.dot(vx_ref[:, j * tq:(j + 1) * tq], jnp.exp2(s - m).astype(BF16),
                      preferred_element_type=F32)
        if j > 0:
            acc = acc_old * jnp.exp2(m_old - m) + acc
        state[iq, h] = (m, acc)
        if j == iq:
            del state[iq, h]
            outs[iq, h] = acc[HEAD_DIM * h:HEAD_DIM * (h + 1), :] / acc[LANES:LANES + 1, :]
            if h == 1:
                o_ref[0, iq * tq:(iq + 1) * tq, :] = jnp.concatenate(
                    [outs.pop((iq, 0)), outs.pop((iq, 1))], axis=0).T

    pending = []
    for step in steps:
        pending.append((step, scores(*step)))
        if len(pending) > FOX_LOOKAHEAD:
            done, s = pending.pop(0)
            absorb(*done, s)
    for done, s in pending:
        absorb(*done, s)


def _fox(k, qt, vt, ft, bf, *, tq):
    B, S, _ = k.shape
    tok = pl.BlockSpec((1, S, LANES), lambda b, p: (b, 0, p))
    feat = lambda rows: pl.BlockSpec((1, rows, S), lambda b, p: (b, p, 0))
    return pl.pallas_call(
        functools.partial(_fox_kernel, seq=S, tq=tq),
        grid=(B, HEAD_PAIRS),
        in_specs=[tok, feat(LANES), feat(LANES), feat(8),
                  pl.BlockSpec((8, S), lambda b, p: (p, 0))],
        out_specs=tok,
        out_shape=jax.ShapeDtypeStruct((B, S, ATT_WIDTH), F32),
        scratch_shapes=[pltpu.VMEM((2, 2 * LANES, S), BF16),
                        pltpu.VMEM((2, S, 2 * LANES), BF16),
                        pltpu.VMEM((LANES + BF16_SUBLANES, S), BF16)],
        compiler_params=_params("parallel", "parallel"),
        name="fox",
    )(k, qt, vt, ft, bf)


MXU_WIDTH = 256
MIX_GROUPS = 2


def _hidden_chunks(d_ff, target):
    assert d_ff % MXU_WIDTH == 0, d_ff
    tiles = d_ff // MXU_WIDTH
    n = -(-d_ff // target)
    sizes = [(tiles // n + (i < tiles % n)) * MXU_WIDTH for i in range(n)]
    starts = [sum(sizes[:i]) for i in range(n)]
    return [slice(a, a + w) for a, w in zip(starts, sizes)]


def _mixffn_kernel(x_ref, ys_ref, yd_ref, yf_ref, gs_ref, gd_ref, gf_ref, wo_ref,
                   gp_ref, g1_ref, wg_ref, wu_ref, wdn_ref, g2_ref, o_ref, *, f_target):
    tm = x_ref.shape[1]
    groups = [slice(i * tm // MIX_GROUPS, (i + 1) * tm // MIX_GROUPS) for i in range(MIX_GROUPS)]
    dot = lambda a, b: jnp.dot(a, b, preferred_element_type=F32)

    def mix_in(r):
        mixed = jnp.concatenate([_rms_scale(ys_ref[r, :], gs_ref[...]).astype(BF16),
                                 _rms_scale(yd_ref[0, r, :], gd_ref[...]).astype(BF16),
                                 _rms_scale(yf_ref[0, r, :], gf_ref[...]).astype(BF16)], axis=1)
        return dot(mixed, wo_ref[...])

    o = [mix_in(r) for r in groups]
    x, h, acc = [], [], []
    chunks = _hidden_chunks(wg_ref.shape[1], f_target)
    gate_up = lambda hg, cols: (dot(hg, wg_ref[:, cols]), dot(hg, wu_ref[:, cols]))
    gu = []
    for g, r in enumerate(groups):
        x.append(x_ref[0, r, :] + _rms_scale(o[g], gp_ref[...]))
        h.append(_rms_scale(x[g], g1_ref[...]).astype(BF16))
        gu.append(gate_up(h[g], chunks[0]))
    for c, cols in enumerate(chunks):
        nxt = []
        for g in range(MIX_GROUPS):
            gate, up = gu[g]
            act = (gate * jax.nn.sigmoid(gate) * up).astype(BF16)
            if c + 1 < len(chunks):
                nxt.append(gate_up(h[g], chunks[c + 1]))
            down = dot(act, wdn_ref[cols, :])
            if c == 0:
                acc.append(down)
            else:
                acc[g] = acc[g] + down
        gu = nxt
    for g, r in enumerate(groups):
        o_ref[0, r, :] = x[g] + _rms_scale(acc[g], g2_ref[...])


def _mixffn(x, ys_sb, yd, yf, gs, gd, gf, wo, gp, g1, wg, wu, wdn, g2, *, tm, f_target):
    B, S, D = x.shape
    const = lambda b, j: (0, 0)
    tok = lambda w: pl.BlockSpec((1, tm, w), lambda b, j: (b, j, 0))
    full = lambda a: pl.BlockSpec(a.shape, const)
    weights = (gs, gd, gf, wo, gp, g1, wg, wu, wdn, g2)
    return pl.pallas_call(
        functools.partial(_mixffn_kernel, f_target=f_target),
        grid=(B, S // tm),
        in_specs=[tok(D), pl.BlockSpec((tm, SSM_WIDTH), lambda b, j: (j, b)), tok(ATT_WIDTH),
                  tok(ATT_WIDTH)] + [full(a) for a in weights],
        out_specs=tok(D),
        out_shape=jax.ShapeDtypeStruct((B, S, D), F32),
        compiler_params=_params("parallel", "parallel"),
        name="mixffn",
    )(x, ys_sb, yd, yf, *weights)


def _s5_matrices(log_dt, a_re, a_im, b_re, b_im, c_re, c_im):
    G, P = a_re.shape
    C = b_re.shape[-1]
    dt = jnp.exp(log_dt)[:, None]
    decay = jnp.exp(a_re * dt)
    abar_re = decay * jnp.cos(a_im * dt)
    abar_im = decay * jnp.sin(a_im * dt)
    mag2 = a_re * a_re + a_im * a_im
    coef_re = ((abar_re - 1.0) * a_re + abar_im * a_im) / mag2
    coef_im = (abar_im * a_re - (abar_re - 1.0) * a_im) / mag2
    bbar_re = coef_re[..., None] * b_re - coef_im[..., None] * b_im
    bbar_im = coef_re[..., None] * b_im + coef_im[..., None] * b_re
    eye = jnp.eye(G, dtype=F32)
    bt_re = jnp.einsum('gpc,gh->gchp', bbar_re, eye).reshape(G * C, G * P)
    bt_im = jnp.einsum('gpc,gh->gchp', bbar_im, eye).reshape(G * C, G * P)
    bt = jnp.concatenate([bt_re, bt_im], axis=1).astype(BF16)
    ct_re = jnp.einsum('gcp,gh->gphc', c_re, eye).reshape(G * P, G * C)
    ct_im = jnp.einsum('gcp,gh->gphc', -c_im, eye).reshape(G * P, G * C)
    ct = jnp.concatenate([ct_re, ct_im], axis=0).astype(BF16)
    return bt, abar_re.reshape(1, G * P), abar_im.reshape(1, G * P), ct


def _per_pair_rows(v):
    out = jnp.zeros((HEAD_PAIRS, 8) + v.shape[1:], v.dtype)
    out = out.at[:, :2].set(v.reshape((HEAD_PAIRS, 2) + v.shape[1:]))
    return out.reshape((HEAD_PAIRS * 8,) + v.shape[1:])


def kernel(x, g_pre_mix, w_in, b_f, ssm_log_dt, ssm_a_re, ssm_a_im, ssm_b_re, ssm_b_im, ssm_c_re, ssm_c_im, ssm_d, w_glu, b_glu, g_mix, w_out, g_post_mix, g_pre_ffn, w_gate_up, w_down, g_post_ffn):
    B, S, D = x.shape
    depth = w_in.shape[0]
    d_ff = w_down.shape[1]
    tm = 512
    o1 = SSM_WIDTH
    o2 = o1 + 3 * ATT_WIDTH
    o3 = o2 + 2 * ATT_WIDTH
    o4 = o3 + ATT_WIDTH
    row = lambda v: v.reshape(1, -1)

    for l in range(depth):
        w = w_in[l]
        u_sb, qkv_dsa, k_fox, qt_fox, vt_fox, ft = _premix(
            x, row(g_pre_mix[l]), w[:, :o1].astype(BF16),
            jnp.concatenate([w[:, o1:o2], w[:, o2 + ATT_WIDTH:o3]], axis=1).astype(BF16),
            w[:, o2:o2 + ATT_WIDTH].T.astype(BF16), w[:, o3:o4].T.astype(BF16),
            _per_pair_rows(w[:, o4:].T).astype(BF16), tm=2 * tm)

        bt, a_re, a_im, ct = _s5_matrices(ssm_log_dt[l], ssm_a_re[l], ssm_a_im[l], ssm_b_re[l],
                                          ssm_b_im[l], ssm_c_re[l], ssm_c_im[l])
        y_ssm = _s5(u_sb.reshape(S * B, SSM_WIDTH), bt, a_re, a_im, ct, row(ssm_d[l]),
                    w_glu[l].astype(BF16), row(b_glu[l]), batch=B, steps=32)

        y_dsa = _dsa(qkv_dsa)

        bf = jnp.broadcast_to(_per_pair_rows(b_f[l][:, None]), (8 * HEAD_PAIRS, S))
        y_fox = _fox(k_fox, qt_fox, vt_fox, ft, bf, tq=256)

        gm = g_mix[l]
        wo = w_out[l].astype(BF16)
        wgu = w_gate_up[l].astype(BF16)
        x = _mixffn(x, y_ssm.reshape(S, B * SSM_WIDTH), y_dsa, y_fox,
                    row(gm[:o1]), row(gm[o1:o1 + ATT_WIDTH]), row(gm[o1 + ATT_WIDTH:]),
                    wo, row(g_post_mix[l]),
                    row(g_pre_ffn[l]), wgu[:, :d_ff], wgu[:, d_ff:], w_down[l].astype(BF16),
                    row(g_post_ffn[l]), tm=tm, f_target=1536)
    return x
```

```python
import functools

import jax
import jax.numpy as jnp
from jax import lax
from jax.experimental import pallas as pl
from jax.experimental.pallas import tpu as pltpu

F32 = jnp.float32
BF16 = jnp.bfloat16

HEAD_DIM = 64
SSM_WIDTH = 256
ATT_WIDTH = 384
N_HEADS = 6
HEAD_PAIRS = N_HEADS // 2
DSA_SPAN = 128
DSA_BLOCK = 128
DSA_DILATIONS = (1, 4, 16)
RMS_EPS = 1e-6
NEG = -1e30
LOG2E = 1.4426950408889634
Q_SCALE = HEAD_DIM ** -0.5 * LOG2E
PRE_GROUPS = 2
DSA_LOOKAHEAD = 3
FOX_LOOKAHEAD = 5

LANES = 128
BF16_SUBLANES = 16
VMEM_LIMIT = 56 * 1024 * 1024

NT_DIMS = (((1,), (1,)), ((), ()))


def _rms_scale(x, g):
    ms = jnp.mean(x * x, axis=-1, keepdims=True)
    return x * lax.rsqrt(ms + RMS_EPS) * g


def _params(*sem):
    return pltpu.CompilerParams(dimension_semantics=sem, vmem_limit_bytes=VMEM_LIMIT)


def _premix_kernel(x_ref, g_ref, wu_ref, wdk_ref, wqt_ref, wvt_ref, wft_ref,
                   u_ref, qd_ref, k_ref, qt_ref, vt_ref, ft_ref):
    tm = x_ref.shape[1]
    groups = [slice(i * tm // PRE_GROUPS, (i + 1) * tm // PRE_GROUPS) for i in range(PRE_GROUPS)]
    hs = [_rms_scale(x_ref[0, r, :], g_ref[...]).astype(BF16) for r in groups]
    for r, h in zip(groups, hs):
        u_ref[0, r, :] = jnp.dot(h, wu_ref[...], preferred_element_type=F32)
        dk = jnp.dot(h, wdk_ref[...], preferred_element_type=F32)
        qd_ref[0, r, :] = dk[:, :3 * ATT_WIDTH]
        k_ref[0, r, :] = dk[:, 3 * ATT_WIDTH:].astype(BF16)
        nt = lambda w_ref: lax.dot_general(w_ref[...], h, NT_DIMS, preferred_element_type=F32)
        qt_ref[0, :, r] = (nt(wqt_ref) * Q_SCALE).astype(BF16)
        vt_ref[0, :, r] = nt(wvt_ref).astype(BF16)
        ft_ref[0, :, r] = nt(wft_ref)


def _premix(x, g, wu, wdk, wqt, wvt, wft, *, tm):
    B, S, D = x.shape
    grid = (B, S // tm)
    const = lambda b, j: (0, 0)
    full = lambda a: pl.BlockSpec(a.shape, const)
    tok = lambda w: pl.BlockSpec((1, tm, w), lambda b, j: (b, j, 0))
    feat = lambda r: pl.BlockSpec((1, r, tm), lambda b, j: (b, 0, j))
    return pl.pallas_call(
        _premix_kernel,
        grid=grid,
        in_specs=[tok(D), full(g), full(wu), full(wdk), full(wqt), full(wvt), full(wft)],
        out_specs=[
            tok(SSM_WIDTH), tok(3 * ATT_WIDTH), tok(ATT_WIDTH), feat(ATT_WIDTH), feat(ATT_WIDTH),
            feat(8 * HEAD_PAIRS),
        ],
        out_shape=[
            jax.ShapeDtypeStruct((B, S, SSM_WIDTH), F32),
            jax.ShapeDtypeStruct((B, S, 3 * ATT_WIDTH), F32),
            jax.ShapeDtypeStruct((B, S, ATT_WIDTH), BF16),
            jax.ShapeDtypeStruct((B, ATT_WIDTH, S), BF16),
            jax.ShapeDtypeStruct((B, ATT_WIDTH, S), BF16),
            jax.ShapeDtypeStruct((B, 8 * HEAD_PAIRS, S), F32),
        ],
        compiler_params=_params("parallel", "parallel"),
        name="premix",
    )(x, g, wu, wdk, wqt, wvt, wft)


def _time_pitch(batch):
    assert batch % 8 == 0, batch
    return batch + 8 if (batch // 8) % 2 == 0 else batch


def _s5_kernel(u_ref, bt_ref, are_ref, aim_ref, ct_ref, d_ref, wg_ref, bg_ref, y_ref,
               tm_ref, xs_ref, h_ref, *, col_tile):
    batch, steps, width = u_ref.shape
    n_state = are_ref.shape[1]
    pitch = _time_pitch(batch)
    lane_tiles = [slice(t * LANES, (t + 1) * LANES) for t in range(width // LANES)]

    @pl.when(pl.program_id(0) == 0)
    def _():
        h_ref[...] = jnp.zeros_like(h_ref)

    for b in range(batch):
        for t, lanes in enumerate(lane_tiles):
            tm_ref[t, pl.ds(b, steps, stride=pitch), :] = u_ref[b, :, lanes]
    u = jnp.concatenate(
        [jnp.concatenate([tm_ref[t, s * pitch:s * pitch + batch, :] for s in range(steps)], axis=0)
         for t in range(len(lane_tiles))], axis=1)
    xs_ref[...] = jnp.dot(u.astype(BF16), bt_ref[...], preferred_element_type=F32)

    for c in range(n_state // col_tile):
        re = slice(c * col_tile, (c + 1) * col_tile)
        im = slice(n_state + c * col_tile, n_state + (c + 1) * col_tile)
        ar = are_ref[:, re]
        ai = aim_ref[:, re]

        def step(s, carry, re=re, im=im, ar=ar, ai=ai):
            hr, hi = carry
            rows = pl.ds(pl.multiple_of(s * batch, batch), batch)
            nr = ar * hr - ai * hi + xs_ref[rows, re]
            ni = ar * hi + ai * hr + xs_ref[rows, im]
            xs_ref[rows, re] = nr
            xs_ref[rows, im] = ni
            return nr, ni

        hr, hi = lax.fori_loop(0, steps, step, (h_ref[:, re], h_ref[:, im]), unroll=2)
        h_ref[:, re] = hr
        h_ref[:, im] = hi

    y = jnp.dot(xs_ref[...].astype(BF16), ct_ref[...], preferred_element_type=F32)
    y = y + d_ref[...] * u
    g = jax.nn.gelu(y)
    z = jnp.dot(g.astype(BF16), wg_ref[...], preferred_element_type=F32) + bg_ref[...]
    out = g * jax.nn.sigmoid(z)

    for t, lanes in enumerate(lane_tiles):
        for s in range(steps):
            tm_ref[t, s * pitch:s * pitch + batch, :] = out[s * batch:(s + 1) * batch, lanes]
    for b in range(batch):
        for t, lanes in enumerate(lane_tiles):
            y_ref[b, :, lanes] = tm_ref[t, pl.ds(b, steps, stride=pitch), :]


def _s5(u, bt, a_re, a_im, ct, d, wg, bg, *, steps):
    B, S, width = u.shape
    n_cols = bt.shape[1]
    const = lambda i: (0, 0)
    full = lambda a: pl.BlockSpec(a.shape, const)
    chunk = pl.BlockSpec((B, steps, width), lambda i: (0, i, 0))
    return pl.pallas_call(
        functools.partial(_s5_kernel, col_tile=2 * LANES),
        grid=(S // steps,),
        in_specs=[chunk, full(bt), full(a_re), full(a_im), full(ct), full(d), full(wg), full(bg)],
        out_specs=chunk,
        out_shape=jax.ShapeDtypeStruct((B, S, width), F32),
        scratch_shapes=[
            pltpu.VMEM((width // LANES, steps * _time_pitch(B), LANES), F32),
            pltpu.VMEM((steps * B, n_cols), F32),
            pltpu.VMEM((B, n_cols), F32),
        ],
        compiler_params=_params("arbitrary"),
        name="s5",
    )(u, bt, a_re, a_im, ct, d, wg, bg)


def _dsa_scores(qb, kw, valid, head0):
    out = []
    for h in range(2):
        qh = jnp.where(head0 if h == 0 else jnp.logical_not(head0), qb, jnp.zeros_like(qb))
        s = lax.dot_general(qh, kw, NT_DIMS, preferred_element_type=F32)
        out.append(jnp.where(valid, s, NEG))
    return out


def _dsa_finish(scores, vw, head0):
    vx = jnp.concatenate([vw, jnp.ones_like(vw)], axis=1)
    parts = []
    for s in scores:
        m = jnp.max(s, axis=-1, keepdims=True)
        r = jnp.dot(jnp.exp2(s - m).astype(BF16), vx, preferred_element_type=F32)
        parts.append((r[:, :LANES], jnp.broadcast_to(m, (s.shape[0], LANES)), r[:, LANES:]))
    return tuple(jnp.where(head0, a, b) for a, b in zip(*parts))


def _softmax_merge(n1, m1, l1, n2, m2, l2):
    m = jnp.maximum(m1, m2)
    e1 = jnp.exp2(m1 - m)
    e2 = jnp.exp2(m2 - m)
    return n1 * e1 + n2 * e2, m, l1 * e1 + l2 * e2


def _dsa_kernel(q_ref, k_ref, v_ref, o_ref, acc_ref, m_ref, l_ref,
                q4_ref, k4_ref, v4_ref, acc4_ref, m4_ref, l4_ref, *, seq):
    blk = DSA_BLOCK
    n_blocks = seq // blk
    head0 = lax.broadcasted_iota(jnp.int32, (blk, LANES), 1) < HEAD_DIM
    qi = lax.broadcasted_iota(jnp.int32, (blk, 2 * blk), 0)
    ki = lax.broadcasted_iota(jnp.int32, (blk, 2 * blk), 1)
    band = (ki >= qi) & (ki <= qi + DSA_SPAN)
    causal = (lax.broadcasted_iota(jnp.int32, (blk, blk), 1)
              <= lax.broadcasted_iota(jnp.int32, (blk, blk), 0))

    def bf(x, scale=None):
        return (x if scale is None else x * scale).astype(BF16)

    pending = []

    def retire():
        scores, vw, sink = pending.pop(0)
        sink(_dsa_finish(scores, vw, head0))

    def submit(q, kw, vw, sink):
        valid = band if kw.shape[0] == 2 * blk else causal
        pending.append((_dsa_scores(bf(q, Q_SCALE), bf(kw), valid, head0), bf(vw), sink))
        if len(pending) > DSA_LOOKAHEAD:
            retire()

    def drain():
        while pending:
            retire()

    def window(ref_rows, start, has_prev):
        return ref_rows(start - blk, 2 * blk) if has_prev else ref_rows(start, blk)

    nat = lambda ref: (lambda start, n: ref[0, start:start + n, :])
    seg = lambda ref: (lambda start, n: ref[start:start + n, :])
    state = (acc_ref, m_ref, l_ref)
    state4 = (acc4_ref, m4_ref, l4_ref)

    d_mid = DSA_DILATIONS[1]
    seg_len = seq // d_mid
    for rho in range(d_mid):
        for src, dst_ref in ((q_ref, q4_ref), (k_ref, k4_ref), (v_ref, v4_ref)):
            dst_ref[rho * seg_len:(rho + 1) * seg_len, :] = src[0, pl.ds(rho, seg_len, stride=d_mid), :]

    def init_state(rows):
        def sink(new):
            for ref, val in zip(state, new):
                ref[rows, :] = val
        return sink

    for n in range(n_blocks):
        start = n * blk
        submit(nat(q_ref)(start, blk), window(nat(k_ref), start, n > 0),
               window(nat(v_ref), start, n > 0), init_state(slice(start, start + blk)))
    drain()

    for rho in range(d_mid):
        for src, dst_ref in zip(state, state4):
            dst_ref[rho * seg_len:(rho + 1) * seg_len, :] = src[pl.ds(rho, seg_len, stride=d_mid), :]

    def merge_state(rows):
        def sink(new):
            old = tuple(ref[rows, :] for ref in state4)
            for ref, val in zip(state4, _softmax_merge(*old, *new)):
                ref[rows, :] = val
        return sink

    for rho in range(d_mid):
        for n in range(seg_len // blk):
            start = rho * seg_len + n * blk
            submit(seg(q4_ref)(start, blk), window(seg(k4_ref), start, n > 0),
                   window(seg(v4_ref), start, n > 0), merge_state(slice(start, start + blk)))
    drain()

    def finalize(rows):
        def sink(new):
            old = tuple(ref[rows, :] for ref in state4)
            num, _, den = _softmax_merge(*old, *new)
            acc4_ref[rows, :] = num / den
        return sink

    inner = DSA_DILATIONS[2] // d_mid
    for rho in range(d_mid):
        for c in range(inner):
            rs = pl.ds(rho * seg_len + c, blk, stride=inner)
            submit(q4_ref[rs, :], k4_ref[rs, :], v4_ref[rs, :], finalize(rs))
    drain()

    for rho in range(d_mid):
        o_ref[0, pl.ds(rho, seg_len, stride=d_mid), :] = acc4_ref[rho * seg_len:(rho + 1) * seg_len, :]


def _dsa(qkv):
    B, S, _ = qkv.shape
    assert S == DSA_DILATIONS[2] * DSA_BLOCK, S
    blockspec = lambda off: pl.BlockSpec((1, S, LANES), lambda b, p: (b, 0, off + p))
    return pl.pallas_call(
        functools.partial(_dsa_kernel, seq=S),
        grid=(B, HEAD_PAIRS),
        in_specs=[blockspec(0), blockspec(HEAD_PAIRS), blockspec(2 * HEAD_PAIRS)],
        out_specs=blockspec(0),
        out_shape=jax.ShapeDtypeStruct((B, S, ATT_WIDTH), F32),
        scratch_shapes=[pltpu.VMEM((S, LANES), F32)] * 9,
        compiler_params=_params("parallel", "parallel"),
        name="dsa",
    )(qkv, qkv, qkv)


def _fox_kernel(k_ref, qt_ref, vt_ref, ft_ref, bf_ref, o_ref, qa_ref, ka_ref, vx_ref, *, seq, tq):
    x = ft_ref[0] + bf_ref[...]
    logf = jnp.minimum(x, 0.0) - jnp.log1p(jnp.exp(-jnp.abs(x)))
    lane_t = lax.broadcasted_iota(jnp.int32, logf.shape, 1)
    cum = logf
    shift = 1
    while shift < seq:
        cum = cum + jnp.where(lane_t >= shift, pltpu.roll(cum, shift, 1), 0.0)
        shift *= 2

    cum = cum * LOG2E
    hi = cum.astype(BF16).astype(F32)
    r1 = cum - hi
    lo = r1.astype(BF16).astype(F32)
    lo2 = (r1 - lo).astype(BF16).astype(F32)

    sub = lax.broadcasted_iota(jnp.int32, (8, seq), 0)
    rows8 = lambda v, r: jnp.broadcast_to(v[r:r + 1, :], (8, seq))
    pad = jnp.zeros((HEAD_DIM - 8, seq), F32)
    q_tiles, k_tiles = [], []
    for r in range(2):
        h8, l8, l28 = rows8(hi, r), rows8(lo, r), rows8(lo2, r)
        q_tiles += [jnp.where(sub == 0, h8, jnp.where(sub == 1, l8, jnp.where(sub == 2, l28,
                              jnp.where(sub < 6, 1.0, 0.0)))), pad]
        k_tiles += [jnp.where(sub < 3, 1.0, jnp.where(sub == 3, -h8, jnp.where(sub == 4, -l8,
                              jnp.where(sub == 5, -l28, 0.0)))), pad]
    aq_t = jnp.concatenate(q_tiles, axis=0).astype(BF16)
    ak = jnp.concatenate(k_tiles, axis=0).T

    lane = lax.broadcasted_iota(jnp.int32, (1, LANES), 1)
    qt = qt_ref[0]
    kv = k_ref[0]
    zeros_t = jnp.zeros((HEAD_DIM, seq), BF16)
    for h in range(2):
        in_head = (lane >= HEAD_DIM * h) & (lane < HEAD_DIM * (h + 1))
        qa_ref[h, :LANES, :] = (jnp.concatenate([qt[:HEAD_DIM], zeros_t], axis=0) if h == 0 else
                                jnp.concatenate([zeros_t, qt[HEAD_DIM:]], axis=0))
        qa_ref[h, LANES:, :] = aq_t
        ka_ref[h, :, :LANES] = kv
        ka_ref[h, :, LANES:] = (ak * in_head.astype(F32)).astype(BF16)

    vx_ref[:LANES, :] = vt_ref[0]
    vx_ref[LANES:, :] = jnp.ones((BF16_SUBLANES, seq), BF16)

    key_le_query = (lax.broadcasted_iota(jnp.int32, (tq, tq), 0)
                    <= lax.broadcasted_iota(jnp.int32, (tq, tq), 1))

    n_q = seq // tq
    steps = [(j, iq, h) for j in range(n_q) for iq in range(j, n_q) for h in range(2)]
    state = {}
    outs = {}

    def scores(j, iq, h):
        s = jnp.dot(ka_ref[h, j * tq:(j + 1) * tq, :], qa_ref[h, :, iq * tq:(iq + 1) * tq],
                    preferred_element_type=F32)
        return jnp.where(key_le_query, s, NEG) if j == iq else s

    def absorb(j, iq, h, s):
        m = jnp.max(s, axis=0, keepdims=True)
        if j > 0:
            m_old, acc_old = state[iq, h]
            m = jnp.maximum(m_old, m)
        acc = jnp.dot(vx_ref[:, j * tq:(j + 1) * tq], jnp.exp2(s - m).astype(BF16),
                      preferred_element_type=F32)
        if j > 0:
            acc = acc_old * jnp.exp2(m_old - m) + acc
        state[iq, h] = (m, acc)
        if j == iq:
            del state[iq, h]
            outs[iq, h] = acc[HEAD_DIM * h:HEAD_DIM * (h + 1), :] / acc[LANES:LANES + 1, :]
            if h == 1:
                o_ref[0, iq * tq:(iq + 1) * tq, :] = jnp.concatenate(
                    [outs.pop((iq, 0)), outs.pop((iq, 1))], axis=0).T

    pending = []
    for step in steps:
        pending.append((step, scores(*step)))
        if len(pending) > FOX_LOOKAHEAD:
            done, s = pending.pop(0)
            absorb(*done, s)
    for done, s in pending:
        absorb(*done, s)


def _fox(k, qt, vt, ft, bf, *, tq):
    B, S, _ = k.shape
    tok = pl.BlockSpec((1, S, LANES), lambda b, p: (b, 0, p))
    feat = lambda rows: pl.BlockSpec((1, rows, S), lambda b, p: (b, p, 0))
    return pl.pallas_call(
        functools.partial(_fox_kernel, seq=S, tq=tq),
        grid=(B, HEAD_PAIRS),
        in_specs=[tok, feat(LANES), feat(LANES), feat(8),
                  pl.BlockSpec((8, S), lambda b, p: (p, 0))],
        out_specs=tok,
        out_shape=jax.ShapeDtypeStruct((B, S, ATT_WIDTH), F32),
        scratch_shapes=[pltpu.VMEM((2, 2 * LANES, S), BF16),
                        pltpu.VMEM((2, S, 2 * LANES), BF16),
                        pltpu.VMEM((LANES + BF16_SUBLANES, S), BF16)],
        compiler_params=_params("parallel", "parallel"),
        name="fox",
    )(k, qt, vt, ft, bf)


MXU_WIDTH = 256
MIX_GROUPS = 2


def _hidden_chunks(d_ff, target):
    assert d_ff % MXU_WIDTH == 0, d_ff
    tiles = d_ff // MXU_WIDTH
    n = -(-d_ff // target)
    sizes = [(tiles // n + (i < tiles % n)) * MXU_WIDTH for i in range(n)]
    starts = [sum(sizes[:i]) for i in range(n)]
    return [slice(a, a + w) for a, w in zip(starts, sizes)]


def _mixffn_kernel(x_ref, ys_ref, yd_ref, yf_ref, gs_ref, gd_ref, gf_ref, wo_ref,
                   gp_ref, g1_ref, wg_ref, wu_ref, wdn_ref, g2_ref, o_ref, *, f_target):
    tm = x_ref.shape[1]
    groups = [slice(i * tm // MIX_GROUPS, (i + 1) * tm // MIX_GROUPS) for i in range(MIX_GROUPS)]
    dot = lambda a, b: jnp.dot(a, b, preferred_element_type=F32)

    def mix_in(r):
        mixed = jnp.concatenate([_rms_scale(ys_ref[0, r, :], gs_ref[...]).astype(BF16),
                                 _rms_scale(yd_ref[0, r, :], gd_ref[...]).astype(BF16),
                                 _rms_scale(yf_ref[0, r, :], gf_ref[...]).astype(BF16)], axis=1)
        return dot(mixed, wo_ref[...])

    o = [mix_in(r) for r in groups]
    x, h, acc = [], [], []
    chunks = _hidden_chunks(wg_ref.shape[1], f_target)
    gate_up = lambda hg, cols: (dot(hg, wg_ref[:, cols]), dot(hg, wu_ref[:, cols]))
    gu = []
    for g, r in enumerate(groups):
        x.append(x_ref[0, r, :] + _rms_scale(o[g], gp_ref[...]))
        h.append(_rms_scale(x[g], g1_ref[...]).astype(BF16))
        gu.append(gate_up(h[g], chunks[0]))
    for c, cols in enumerate(chunks):
        nxt = []
        for g in range(MIX_GROUPS):
            gate, up = gu[g]
            act = (gate * jax.nn.sigmoid(gate) * up).astype(BF16)
            if c + 1 < len(chunks):
                nxt.append(gate_up(h[g], chunks[c + 1]))
            down = dot(act, wdn_ref[cols, :])
            if c == 0:
                acc.append(down)
            else:
                acc[g] = acc[g] + down
        gu = nxt
    for g, r in enumerate(groups):
        o_ref[0, r, :] = x[g] + _rms_scale(acc[g], g2_ref[...])


def _mixffn(x, ys_sb, yd, yf, gs, gd, gf, wo, gp, g1, wg, wu, wdn, g2, *, tm, f_target):
    B, S, D = x.shape
    const = lambda b, j: (0, 0)
    tok = lambda w: pl.BlockSpec((1, tm, w), lambda b, j: (b, j, 0))
    full = lambda a: pl.BlockSpec(a.shape, const)
    weights = (gs, gd, gf, wo, gp, g1, wg, wu, wdn, g2)
    return pl.pallas_call(
        functools.partial(_mixffn_kernel, f_target=f_target),
        grid=(B, S // tm),
        in_specs=[tok(D), tok(SSM_WIDTH), tok(ATT_WIDTH),
                  tok(ATT_WIDTH)] + [full(a) for a in weights],
        out_specs=tok(D),
        out_shape=jax.ShapeDtypeStruct((B, S, D), F32),
        compiler_params=_params("parallel", "parallel"),
        name="mixffn",
    )(x, ys_sb, yd, yf, *weights)


def _s5_matrices(log_dt, a_re, a_im, b_re, b_im, c_re, c_im):
    G, P = a_re.shape
    C = b_re.shape[-1]
    dt = jnp.exp(log_dt)[:, None]
    decay = jnp.exp(a_re * dt)
    abar_re = decay * jnp.cos(a_im * dt)
    abar_im = decay * jnp.sin(a_im * dt)
    mag2 = a_re * a_re + a_im * a_im
    coef_re = ((abar_re - 1.0) * a_re + abar_im * a_im) / mag2
    coef_im = (abar_im * a_re - (abar_re - 1.0) * a_im) / mag2
    bbar_re = coef_re[..., None] * b_re - coef_im[..., None] * b_im
    bbar_im = coef_re[..., None] * b_im + coef_im[..., None] * b_re
    eye = jnp.eye(G, dtype=F32)
    bt_re = jnp.einsum('gpc,gh->gchp', bbar_re, eye).reshape(G * C, G * P)
    bt_im = jnp.einsum('gpc,gh->gchp', bbar_im, eye).reshape(G * C, G * P)
    bt = jnp.concatenate([bt_re, bt_im], axis=1).astype(BF16)
    ct_re = jnp.einsum('gcp,gh->gphc', c_re, eye).reshape(G * P, G * C)
    ct_im = jnp.einsum('gcp,gh->gphc', -c_im, eye).reshape(G * P, G * C)
    ct = jnp.concatenate([ct_re, ct_im], axis=0).astype(BF16)
    return bt, abar_re.reshape(1, G * P), abar_im.reshape(1, G * P), ct


def _per_pair_rows(v):
    out = jnp.zeros((HEAD_PAIRS, 8) + v.shape[1:], v.dtype)
    out = out.at[:, :2].set(v.reshape((HEAD_PAIRS, 2) + v.shape[1:]))
    return out.reshape((HEAD_PAIRS * 8,) + v.shape[1:])


def kernel(x, g_pre_mix, w_in, b_f, ssm_log_dt, ssm_a_re, ssm_a_im, ssm_b_re, ssm_b_im, ssm_c_re, ssm_c_im, ssm_d, w_glu, b_glu, g_mix, w_out, g_post_mix, g_pre_ffn, w_gate_up, w_down, g_post_ffn):
    B, S, D = x.shape
    depth = w_in.shape[0]
    d_ff = w_down.shape[1]
    tm = 512
    o1 = SSM_WIDTH
    o2 = o1 + 3 * ATT_WIDTH
    o3 = o2 + 2 * ATT_WIDTH
    o4 = o3 + ATT_WIDTH
    row = lambda v: v.reshape(1, -1)

    for l in range(depth):
        w = w_in[l]
        u_ssm, qkv_dsa, k_fox, qt_fox, vt_fox, ft = _premix(
            x, row(g_pre_mix[l]), w[:, :o1].astype(BF16),
            jnp.concatenate([w[:, o1:o2], w[:, o2 + ATT_WIDTH:o3]], axis=1).astype(BF16),
            w[:, o2:o2 + ATT_WIDTH].T.astype(BF16), w[:, o3:o4].T.astype(BF16),
            _per_pair_rows(w[:, o4:].T).astype(BF16), tm=2 * tm)

        bt, a_re, a_im, ct = _s5_matrices(ssm_log_dt[l], ssm_a_re[l], ssm_a_im[l], ssm_b_re[l],
                                          ssm_b_im[l], ssm_c_re[l], ssm_c_im[l])
        y_ssm = _s5(u_ssm, bt, a_re, a_im, ct, row(ssm_d[l]), w_glu[l].astype(BF16),
                    row(b_glu[l]), steps=64)

        y_dsa = _dsa(qkv_dsa)

        bf = jnp.broadcast_to(_per_pair_rows(b_f[l][:, None]), (8 * HEAD_PAIRS, S))
        y_fox = _fox(k_fox, qt_fox, vt_fox, ft, bf, tq=256)

        gm = g_mix[l]
        wo = w_out[l].astype(BF16)
        wgu = w_gate_up[l].astype(BF16)
        x = _mixffn(x, y_ssm, y_dsa, y_fox,
                    row(gm[:o1]), row(gm[o1:o1 + ATT_WIDTH]), row(gm[o1 + ATT_WIDTH:]),
                    wo, row(g_post_mix[l]),
                    row(g_pre_ffn[l]), wgu[:, :d_ff], wgu[:, d_ff:], w_down[l].astype(BF16),
                    row(g_post_ffn[l]), tm=tm, f_target=1536)
    return x
```

```python
import functools

import jax
import jax.numpy as jnp
from jax import lax
from jax.experimental import pallas as pl
from jax.experimental.pallas import tpu as pltpu

F32 = jnp.float32
BF16 = jnp.bfloat16

HEAD_DIM = 64
SSM_WIDTH = 256
ATT_WIDTH = 384
N_HEADS = 6
HEAD_PAIRS = N_HEADS // 2
DSA_SPAN = 128
DSA_BLOCK = 128
DSA_DILATIONS = (1, 4, 16)
RMS_EPS = 1e-6
NEG = -1e30
LOG2E = 1.4426950408889634
Q_SCALE = HEAD_DIM ** -0.5 * LOG2E
S5_READ_GROUPS = 4
PRE_GROUPS = 2
DSA_LOOKAHEAD = 3
FOX_LOOKAHEAD = 5

LANES = 128
BF16_SUBLANES = 16
VMEM_LIMIT = 56 * 1024 * 1024

NT_DIMS = (((1,), (1,)), ((), ()))


def _rms_scale(x, g):
    ms = jnp.mean(x * x, axis=-1, keepdims=True)
    return x * lax.rsqrt(ms + RMS_EPS) * g


def _params(*sem):
    return pltpu.CompilerParams(dimension_semantics=sem, vmem_limit_bytes=VMEM_LIMIT)


def _premix_kernel(x_ref, g_ref, wu_ref, wdk_ref, wqt_ref, wvt_ref, wft_ref,
                   u_ref, qd_ref, k_ref, qt_ref, vt_ref, ft_ref):
    tm = x_ref.shape[1]
    groups = [slice(i * tm // PRE_GROUPS, (i + 1) * tm // PRE_GROUPS) for i in range(PRE_GROUPS)]
    hs = [_rms_scale(x_ref[0, r, :], g_ref[...]).astype(BF16) for r in groups]
    for r, h in zip(groups, hs):
        u_ref[0, r, :] = jnp.dot(h, wu_ref[...], preferred_element_type=F32)
        dk = jnp.dot(h, wdk_ref[...], preferred_element_type=F32)
        qd_ref[0, r, :] = dk[:, :3 * ATT_WIDTH]
        k_ref[0, r, :] = dk[:, 3 * ATT_WIDTH:].astype(BF16)
        nt = lambda w_ref: lax.dot_general(w_ref[...], h, NT_DIMS, preferred_element_type=F32)
        qt_ref[0, :, r] = (nt(wqt_ref) * Q_SCALE).astype(BF16)
        vt_ref[0, :, r] = nt(wvt_ref).astype(BF16)
        ft_ref[0, :, r] = nt(wft_ref)


def _premix(x, g, wu, wdk, wqt, wvt, wft, *, tm):
    B, S, D = x.shape
    grid = (B, S // tm)
    const = lambda b, j: (0, 0)
    full = lambda a: pl.BlockSpec(a.shape, const)
    tok = lambda w: pl.BlockSpec((1, tm, w), lambda b, j: (b, j, 0))
    feat = lambda r: pl.BlockSpec((1, r, tm), lambda b, j: (b, 0, j))
    return pl.pallas_call(
        _premix_kernel,
        grid=grid,
        in_specs=[tok(D), full(g), full(wu), full(wdk), full(wqt), full(wvt), full(wft)],
        out_specs=[
            tok(SSM_WIDTH), tok(3 * ATT_WIDTH), tok(ATT_WIDTH), feat(ATT_WIDTH), feat(ATT_WIDTH),
            feat(8 * HEAD_PAIRS),
        ],
        out_shape=[
            jax.ShapeDtypeStruct((B, S, SSM_WIDTH), F32),
            jax.ShapeDtypeStruct((B, S, 3 * ATT_WIDTH), F32),
            jax.ShapeDtypeStruct((B, S, ATT_WIDTH), BF16),
            jax.ShapeDtypeStruct((B, ATT_WIDTH, S), BF16),
            jax.ShapeDtypeStruct((B, ATT_WIDTH, S), BF16),
            jax.ShapeDtypeStruct((B, 8 * HEAD_PAIRS, S), F32),
        ],
        compiler_params=_params("parallel", "parallel"),
        name="premix",
    )(x, g, wu, wdk, wqt, wvt, wft)


def _time_pitch(batch):
    assert batch % 8 == 0, batch
    return batch + 8 if (batch // 8) % 2 == 0 else batch


def _s5_kernel(u_ref, bt_ref, are_ref, aim_ref, ct_ref, d_ref, wg_ref, bg_ref, y_ref,
               tm_ref, xs_ref, h_ref, *, col_tile):
    batch, steps, width = u_ref.shape
    n_state = are_ref.shape[1]
    pitch = _time_pitch(batch)
    lane_tiles = [slice(t * LANES, (t + 1) * LANES) for t in range(width // LANES)]

    @pl.when(pl.program_id(0) == 0)
    def _():
        h_ref[...] = jnp.zeros_like(h_ref)

    for b in range(batch):
        for t, lanes in enumerate(lane_tiles):
            tm_ref[t, pl.ds(b, steps, stride=pitch), :] = u_ref[b, :, lanes]
    u = jnp.concatenate(
        [jnp.concatenate([tm_ref[t, s * pitch:s * pitch + batch, :] for s in range(steps)], axis=0)
         for t in range(len(lane_tiles))], axis=1)
    group_rows = steps // S5_READ_GROUPS * batch
    for gi in range(S5_READ_GROUPS):
        rows = slice(gi * group_rows, (gi + 1) * group_rows)
        xs_ref[rows, :] = jnp.dot(u[rows, :].astype(BF16), bt_ref[...], preferred_element_type=F32)

    for c in range(n_state // col_tile):
        re = slice(c * col_tile, (c + 1) * col_tile)
        im = slice(n_state + c * col_tile, n_state + (c + 1) * col_tile)
        ar = are_ref[:, re]
        ai = aim_ref[:, re]

        def step(s, carry, re=re, im=im, ar=ar, ai=ai):
            hr, hi = carry
            rows = pl.ds(pl.multiple_of(s * batch, batch), batch)
            nr = ar * hr - ai * hi + xs_ref[rows, re]
            ni = ar * hi + ai * hr + xs_ref[rows, im]
            xs_ref[rows, re] = nr
            xs_ref[rows, im] = ni
            return nr, ni

        hr, hi = lax.fori_loop(0, steps, step, (h_ref[:, re], h_ref[:, im]), unroll=2)
        h_ref[:, re] = hr
        h_ref[:, im] = hi

    group_steps = steps // S5_READ_GROUPS

    def project(gi):
        rows = slice(gi * group_steps * batch, (gi + 1) * group_steps * batch)
        y = jnp.dot(xs_ref[rows, :].astype(BF16), ct_ref[...], preferred_element_type=F32)
        return y + d_ref[...] * u[rows, :]

    def finish(gi, y):
        g = jax.nn.gelu(y)
        z = jnp.dot(g.astype(BF16), wg_ref[...], preferred_element_type=F32) + bg_ref[...]
        out = g * jax.nn.sigmoid(z)
        for t, lanes in enumerate(lane_tiles):
            for k in range(group_steps):
                s = gi * group_steps + k
                tm_ref[t, s * pitch:s * pitch + batch, :] = out[k * batch:(k + 1) * batch, lanes]

    y_next = project(0)
    for gi in range(S5_READ_GROUPS):
        y_cur = y_next
        if gi + 1 < S5_READ_GROUPS:
            y_next = project(gi + 1)
        finish(gi, y_cur)

    for b in range(batch):
        for t, lanes in enumerate(lane_tiles):
            y_ref[b, :, lanes] = tm_ref[t, pl.ds(b, steps, stride=pitch), :]


def _s5(u, bt, a_re, a_im, ct, d, wg, bg, *, steps):
    B, S, width = u.shape
    n_cols = bt.shape[1]
    const = lambda i: (0, 0)
    full = lambda a: pl.BlockSpec(a.shape, const)
    chunk = pl.BlockSpec((B, steps, width), lambda i: (0, i, 0))
    return pl.pallas_call(
        functools.partial(_s5_kernel, col_tile=2 * LANES),
        grid=(S // steps,),
        in_specs=[chunk, full(bt), full(a_re), full(a_im), full(ct), full(d), full(wg), full(bg)],
        out_specs=chunk,
        out_shape=jax.ShapeDtypeStruct((B, S, width), F32),
        scratch_shapes=[
            pltpu.VMEM((width // LANES, steps * _time_pitch(B), LANES), F32),
            pltpu.VMEM((steps * B, n_cols), F32),
            pltpu.VMEM((B, n_cols), F32),
        ],
        compiler_params=_params("arbitrary"),
        name="s5",
    )(u, bt, a_re, a_im, ct, d, wg, bg)


def _dsa_scores(qb, kw, valid, head0):
    out = []
    for h in range(2):
        qh = jnp.where(head0 if h == 0 else jnp.logical_not(head0), qb, jnp.zeros_like(qb))
        s = lax.dot_general(qh, kw, NT_DIMS, preferred_element_type=F32)
        out.append(jnp.where(valid, s, NEG))
    return out


def _dsa_finish(scores, vw, head0):
    vx = jnp.concatenate([vw, jnp.ones_like(vw)], axis=1)
    parts = []
    for s in scores:
        m = jnp.max(s, axis=-1, keepdims=True)
        r = jnp.dot(jnp.exp2(s - m).astype(BF16), vx, preferred_element_type=F32)
        parts.append((r[:, :LANES], jnp.broadcast_to(m, (s.shape[0], LANES)), r[:, LANES:]))
    return tuple(jnp.where(head0, a, b) for a, b in zip(*parts))


def _softmax_merge(n1, m1, l1, n2, m2, l2):
    m = jnp.maximum(m1, m2)
    e1 = jnp.exp2(m1 - m)
    e2 = jnp.exp2(m2 - m)
    return n1 * e1 + n2 * e2, m, l1 * e1 + l2 * e2


def _dsa_kernel(q_ref, k_ref, v_ref, o_ref, acc_ref, m_ref, l_ref,
                q4_ref, k4_ref, v4_ref, acc4_ref, m4_ref, l4_ref, *, seq):
    blk = DSA_BLOCK
    n_blocks = seq // blk
    head0 = lax.broadcasted_iota(jnp.int32, (blk, LANES), 1) < HEAD_DIM
    qi = lax.broadcasted_iota(jnp.int32, (blk, 2 * blk), 0)
    ki = lax.broadcasted_iota(jnp.int32, (blk, 2 * blk), 1)
    band = (ki >= qi) & (ki <= qi + DSA_SPAN)
    causal = (lax.broadcasted_iota(jnp.int32, (blk, blk), 1)
              <= lax.broadcasted_iota(jnp.int32, (blk, blk), 0))

    def bf(x, scale=None):
        return (x if scale is None else x * scale).astype(BF16)

    pending = []

    def retire():
        scores, vw, sink = pending.pop(0)
        sink(_dsa_finish(scores, vw, head0))

    def submit(q, kw, vw, sink):
        valid = band if kw.shape[0] == 2 * blk else causal
        pending.append((_dsa_scores(bf(q, Q_SCALE), bf(kw), valid, head0), bf(vw), sink))
        if len(pending) > DSA_LOOKAHEAD:
            retire()

    def drain():
        while pending:
            retire()

    def window(ref_rows, start, has_prev):
        return ref_rows(start - blk, 2 * blk) if has_prev else ref_rows(start, blk)

    nat = lambda ref: (lambda start, n: ref[0, start:start + n, :])
    seg = lambda ref: (lambda start, n: ref[start:start + n, :])
    state = (acc_ref, m_ref, l_ref)
    state4 = (acc4_ref, m4_ref, l4_ref)

    d_mid = DSA_DILATIONS[1]
    seg_len = seq // d_mid
    for rho in range(d_mid):
        for src, dst_ref in ((q_ref, q4_ref), (k_ref, k4_ref), (v_ref, v4_ref)):
            dst_ref[rho * seg_len:(rho + 1) * seg_len, :] = src[0, pl.ds(rho, seg_len, stride=d_mid), :]

    def init_state(rows):
        def sink(new):
            for ref, val in zip(state, new):
                ref[rows, :] = val
        return sink

    for n in range(n_blocks):
        start = n * blk
        submit(nat(q_ref)(start, blk), window(nat(k_ref), start, n > 0),
               window(nat(v_ref), start, n > 0), init_state(slice(start, start + blk)))
    drain()

    for rho in range(d_mid):
        for src, dst_ref in zip(state, state4):
            dst_ref[rho * seg_len:(rho + 1) * seg_len, :] = src[pl.ds(rho, seg_len, stride=d_mid), :]

    def merge_state(rows):
        def sink(new):
            old = tuple(ref[rows, :] for ref in state4)
            for ref, val in zip(state4, _softmax_merge(*old, *new)):
                ref[rows, :] = val
        return sink

    for rho in range(d_mid):
        for n in range(seg_len // blk):
            start = rho * seg_len + n * blk
            submit(seg(q4_ref)(start, blk), window(seg(k4_ref), start, n > 0),
                   window(seg(v4_ref), start, n > 0), merge_state(slice(start, start + blk)))
    drain()

    def finalize(rows):
        def sink(new):
            old = tuple(ref[rows, :] for ref in state4)
            num, _, den = _softmax_merge(*old, *new)
            acc4_ref[rows, :] = num / den
        return sink

    inner = DSA_DILATIONS[2] // d_mid
    for rho in range(d_mid):
        for c in range(inner):
            rs = pl.ds(rho * seg_len + c, blk, stride=inner)
            submit(q4_ref[rs, :], k4_ref[rs, :], v4_ref[rs, :], finalize(rs))
    drain()

    for rho in range(d_mid):
        o_ref[0, pl.ds(rho, seg_len, stride=d_mid), :] = acc4_ref[rho * seg_len:(rho + 1) * seg_len, :]


def _dsa(qkv):
    B, S, _ = qkv.shape
    assert S == DSA_DILATIONS[2] * DSA_BLOCK, S
    blockspec = lambda off: pl.BlockSpec((1, S, LANES), lambda b, p: (b, 0, off + p))
    return pl.pallas_call(
        functools.partial(_dsa_kernel, seq=S),
        grid=(B, HEAD_PAIRS),
        in_specs=[blockspec(0), blockspec(HEAD_PAIRS), blockspec(2 * HEAD_PAIRS)],
        out_specs=blockspec(0),
        out_shape=jax.ShapeDtypeStruct((B, S, ATT_WIDTH), F32),
        scratch_shapes=[pltpu.VMEM((S, LANES), F32)] * 9,
        compiler_params=_params("parallel", "parallel"),
        name="dsa",
    )(qkv, qkv, qkv)


def _fox_kernel(k_ref, qt_ref, vt_ref, ft_ref, bf_ref, o_ref, qa_ref, ka_ref, vx_ref, *, seq, tq):
    x = ft_ref[0] + bf_ref[...]
    logf = jnp.minimum(x, 0.0) - jnp.log1p(jnp.exp(-jnp.abs(x)))
    lane_t = lax.broadcasted_iota(jnp.int32, logf.shape, 1)
    cum = logf
    shift = 1
    while shift < seq:
        cum = cum + jnp.where(lane_t >= shift, pltpu.roll(cum, shift, 1), 0.0)
        shift *= 2

    cum = cum * LOG2E
    hi = cum.astype(BF16).astype(F32)
    r1 = cum - hi
    lo = r1.astype(BF16).astype(F32)
    lo2 = (r1 - lo).astype(BF16).astype(F32)

    sub = lax.broadcasted_iota(jnp.int32, (8, seq), 0)
    rows8 = lambda v, r: jnp.broadcast_to(v[r:r + 1, :], (8, seq))
    pad = jnp.zeros((HEAD_DIM - 8, seq), F32)
    q_tiles, k_tiles = [], []
    for r in range(2):
        h8, l8, l28 = rows8(hi, r), rows8(lo, r), rows8(lo2, r)
        q_tiles += [jnp.where(sub == 0, h8, jnp.where(sub == 1, l8, jnp.where(sub == 2, l28,
                              jnp.where(sub < 6, 1.0, 0.0)))), pad]
        k_tiles += [jnp.where(sub < 3, 1.0, jnp.where(sub == 3, -h8, jnp.where(sub == 4, -l8,
                              jnp.where(sub == 5, -l28, 0.0)))), pad]
    aq_t = jnp.concatenate(q_tiles, axis=0).astype(BF16)
    ak = jnp.concatenate(k_tiles, axis=0).T

    lane = lax.broadcasted_iota(jnp.int32, (1, LANES), 1)
    qt = qt_ref[0]
    kv = k_ref[0]
    zeros_t = jnp.zeros((HEAD_DIM, seq), BF16)
    for h in range(2):
        in_head = (lane >= HEAD_DIM * h) & (lane < HEAD_DIM * (h + 1))
        qa_ref[h, :LANES, :] = (jnp.concatenate([qt[:HEAD_DIM], zeros_t], axis=0) if h == 0 else
                                jnp.concatenate([zeros_t, qt[HEAD_DIM:]], axis=0))
        qa_ref[h, LANES:, :] = aq_t
        ka_ref[h, :, :LANES] = kv
        ka_ref[h, :, LANES:] = (ak * in_head.astype(F32)).astype(BF16)

    for h in range(2):
        vx_ref[h, :HEAD_DIM, :] = vt_ref[0, HEAD_DIM * h:HEAD_DIM * (h + 1), :]
        vx_ref[h, HEAD_DIM:, :] = jnp.ones((BF16_SUBLANES, seq), BF16)

    key_le_query = (lax.broadcasted_iota(jnp.int32, (tq, tq), 0)
                    <= lax.broadcasted_iota(jnp.int32, (tq, tq), 1))

    n_q = seq // tq
    steps = [(j, iq, h) for j in range(n_q) for iq in range(j, n_q) for h in range(2)]
    state = {}
    outs = {}

    def scores(j, iq, h):
        s = jnp.dot(ka_ref[h, j * tq:(j + 1) * tq, :], qa_ref[h, :, iq * tq:(iq + 1) * tq],
                    preferred_element_type=F32)
        return jnp.where(key_le_query, s, NEG) if j == iq else s

    def absorb(j, iq, h, s):
        m = jnp.max(s, axis=0, keepdims=True)
        if j > 0:
            m_old, acc_old = state[iq, h]
            m = jnp.maximum(m_old, m)
        acc = jnp.dot(vx_ref[h, :, j * tq:(j + 1) * tq], jnp.exp2(s - m).astype(BF16),
                      preferred_element_type=F32)
        if j > 0:
            acc = acc_old * jnp.exp2(m_old - m) + acc
        state[iq, h] = (m, acc)
        if j == iq:
            del state[iq, h]
            outs[iq, h] = acc[:HEAD_DIM, :] / acc[HEAD_DIM:HEAD_DIM + 1, :]
            if h == 1:
                o_ref[0, iq * tq:(iq + 1) * tq, :] = jnp.concatenate(
                    [outs.pop((iq, 0)), outs.pop((iq, 1))], axis=0).T

    pending = []
    for step in steps:
        pending.append((step, scores(*step)))
        if len(pending) > FOX_LOOKAHEAD:
            done, s = pending.pop(0)
            absorb(*done, s)
    for done, s in pending:
        absorb(*done, s)


def _fox(k, qt, vt, ft, bf, *, tq):
    B, S, _ = k.shape
    tok = pl.BlockSpec((1, S, LANES), lambda b, p: (b, 0, p))
    feat = lambda rows: pl.BlockSpec((1, rows, S), lambda b, p: (b, p, 0))
    return pl.pallas_call(
        functools.partial(_fox_kernel, seq=S, tq=tq),
        grid=(B, HEAD_PAIRS),
        in_specs=[tok, feat(LANES), feat(LANES), feat(8),
                  pl.BlockSpec((8, S), lambda b, p: (p, 0))],
        out_specs=tok,
        out_shape=jax.ShapeDtypeStruct((B, S, ATT_WIDTH), F32),
        scratch_shapes=[pltpu.VMEM((2, 2 * LANES, S), BF16),
                        pltpu.VMEM((2, S, 2 * LANES), BF16),
                        pltpu.VMEM((2, HEAD_DIM + BF16_SUBLANES, S), BF16)],
        compiler_params=_params("parallel", "parallel"),
        name="fox",
    )(k, qt, vt, ft, bf)


MXU_WIDTH = 256
MIX_GROUPS = 2


def _hidden_chunks(d_ff, target):
    assert d_ff % MXU_WIDTH == 0, d_ff
    tiles = d_ff // MXU_WIDTH
    n = -(-d_ff // target)
    sizes = [(tiles // n + (i < tiles % n)) * MXU_WIDTH for i in range(n)]
    starts = [sum(sizes[:i]) for i in range(n)]
    return [slice(a, a + w) for a, w in zip(starts, sizes)]


def _mixffn_kernel(x_ref, ys_ref, yd_ref, yf_ref, gs_ref, gd_ref, gf_ref, wo_ref,
                   gp_ref, g1_ref, wg_ref, wu_ref, wdn_ref, g2_ref, o_ref, *, f_target):
    tm = x_ref.shape[1]
    groups = [slice(i * tm // MIX_GROUPS, (i + 1) * tm // MIX_GROUPS) for i in range(MIX_GROUPS)]
    dot = lambda a, b: jnp.dot(a, b, preferred_element_type=F32)

    def mix_in(r):
        mixed = jnp.concatenate([_rms_scale(ys_ref[0, r, :], gs_ref[...]).astype(BF16),
                                 _rms_scale(yd_ref[0, r, :], gd_ref[...]).astype(BF16),
                                 _rms_scale(yf_ref[0, r, :], gf_ref[...]).astype(BF16)], axis=1)
        return dot(mixed, wo_ref[...])

    o = [mix_in(r) for r in groups]
    x, h, acc = [], [], []
    chunks = _hidden_chunks(wg_ref.shape[1], f_target)
    gate_up = lambda hg, cols: (dot(hg, wg_ref[:, cols]), dot(hg, wu_ref[:, cols]))
    gu = []
    for g, r in enumerate(groups):
        x.append(x_ref[0, r, :] + _rms_scale(o[g], gp_ref[...]))
        h.append(_rms_scale(x[g], g1_ref[...]).astype(BF16))
        gu.append(gate_up(h[g], chunks[0]))
    for c, cols in enumerate(chunks):
        nxt = []
        for g in range(MIX_GROUPS):
            gate, up = gu[g]
            act = (gate * jax.nn.sigmoid(gate) * up).astype(BF16)
            if c + 1 < len(chunks):
                nxt.append(gate_up(h[g], chunks[c + 1]))
            down = dot(act, wdn_ref[cols, :])
            if c == 0:
                acc.append(down)
            else:
                acc[g] = acc[g] + down
        gu = nxt
    for g, r in enumerate(groups):
        o_ref[0, r, :] = x[g] + _rms_scale(acc[g], g2_ref[...])


def _mixffn(x, ys_sb, yd, yf, gs, gd, gf, wo, gp, g1, wg, wu, wdn, g2, *, tm, f_target):
    B, S, D = x.shape
    const = lambda b, j: (0, 0)
    tok = lambda w: pl.BlockSpec((1, tm, w), lambda b, j: (b, j, 0))
    full = lambda a: pl.BlockSpec(a.shape, const)
    weights = (gs, gd, gf, wo, gp, g1, wg, wu, wdn, g2)
    return pl.pallas_call(
        functools.partial(_mixffn_kernel, f_target=f_target),
        grid=(B, S // tm),
        in_specs=[tok(D), tok(SSM_WIDTH), tok(ATT_WIDTH),
                  tok(ATT_WIDTH)] + [full(a) for a in weights],
        out_specs=tok(D),
        out_shape=jax.ShapeDtypeStruct((B, S, D), F32),
        compiler_params=_params("parallel", "parallel"),
        name="mixffn",
    )(x, ys_sb, yd, yf, *weights)


def _s5_matrices(log_dt, a_re, a_im, b_re, b_im, c_re, c_im):
    G, P = a_re.shape
    C = b_re.shape[-1]
    dt = jnp.exp(log_dt)[:, None]
    decay = jnp.exp(a_re * dt)
    abar_re = decay * jnp.cos(a_im * dt)
    abar_im = decay * jnp.sin(a_im * dt)
    mag2 = a_re * a_re + a_im * a_im
    coef_re = ((abar_re - 1.0) * a_re + abar_im * a_im) / mag2
    coef_im = (abar_im * a_re - (abar_re - 1.0) * a_im) / mag2
    bbar_re = coef_re[..., None] * b_re - coef_im[..., None] * b_im
    bbar_im = coef_re[..., None] * b_im + coef_im[..., None] * b_re
    eye = jnp.eye(G, dtype=F32)
    bt_re = jnp.einsum('gpc,gh->gchp', bbar_re, eye).reshape(G * C, G * P)
    bt_im = jnp.einsum('gpc,gh->gchp', bbar_im, eye).reshape(G * C, G * P)
    bt = jnp.concatenate([bt_re, bt_im], axis=1).astype(BF16)
    ct_re = jnp.einsum('gcp,gh->gphc', c_re, eye).reshape(G * P, G * C)
    ct_im = jnp.einsum('gcp,gh->gphc', -c_im, eye).reshape(G * P, G * C)
    ct = jnp.concatenate([ct_re, ct_im], axis=0).astype(BF16)
    return bt, abar_re.reshape(1, G * P), abar_im.reshape(1, G * P), ct


def _per_pair_rows(v):
    out = jnp.zeros((HEAD_PAIRS, 8) + v.shape[1:], v.dtype)
    out = out.at[:, :2].set(v.reshape((HEAD_PAIRS, 2) + v.shape[1:]))
    return out.reshape((HEAD_PAIRS * 8,) + v.shape[1:])


def kernel(x, g_pre_mix, w_in, b_f, ssm_log_dt, ssm_a_re, ssm_a_im, ssm_b_re, ssm_b_im, ssm_c_re, ssm_c_im, ssm_d, w_glu, b_glu, g_mix, w_out, g_post_mix, g_pre_ffn, w_gate_up, w_down, g_post_ffn):
    B, S, D = x.shape
    depth = w_in.shape[0]
    d_ff = w_down.shape[1]
    tm = 512
    o1 = SSM_WIDTH
    o2 = o1 + 3 * ATT_WIDTH
    o3 = o2 + 2 * ATT_WIDTH
    o4 = o3 + ATT_WIDTH
    row = lambda v: v.reshape(1, -1)

    for l in range(depth):
        w = w_in[l]
        u_ssm, qkv_dsa, k_fox, qt_fox, vt_fox, ft = _premix(
            x, row(g_pre_mix[l]), w[:, :o1].astype(BF16),
            jnp.concatenate([w[:, o1:o2], w[:, o2 + ATT_WIDTH:o3]], axis=1).astype(BF16),
            w[:, o2:o2 + ATT_WIDTH].T.astype(BF16), w[:, o3:o4].T.astype(BF16),
            _per_pair_rows(w[:, o4:].T).astype(BF16), tm=2 * tm)

        bt, a_re, a_im, ct = _s5_matrices(ssm_log_dt[l], ssm_a_re[l], ssm_a_im[l], ssm_b_re[l],
                                          ssm_b_im[l], ssm_c_re[l], ssm_c_im[l])
        y_ssm = _s5(u_ssm, bt, a_re, a_im, ct, row(ssm_d[l]), w_glu[l].astype(BF16),
                    row(b_glu[l]), steps=64)

        y_dsa = _dsa(qkv_dsa)

        bf = jnp.broadcast_to(_per_pair_rows(b_f[l][:, None]), (8 * HEAD_PAIRS, S))
        y_fox = _fox(k_fox, qt_fox, vt_fox, ft, bf, tq=256)

        gm = g_mix[l]
        wo = w_out[l].astype(BF16)
        wgu = w_gate_up[l].astype(BF16)
        x = _mixffn(x, y_ssm, y_dsa, y_fox,
                    row(gm[:o1]), row(gm[o1:o1 + ATT_WIDTH]), row(gm[o1 + ATT_WIDTH:]),
                    wo, row(g_post_mix[l]),
                    row(g_pre_ffn[l]), wgu[:, :d_ff], wgu[:, d_ff:], w_down[l].astype(BF16),
                    row(g_post_ffn[l]), tm=tm, f_target=1536)
    return x
```

```python
import functools

import jax
import jax.numpy as jnp
from jax import lax
from jax.experimental import pallas as pl
from jax.experimental.pallas import tpu as pltpu

F32 = jnp.float32
BF16 = jnp.bfloat16

HEAD_DIM = 64
SSM_WIDTH = 256
ATT_WIDTH = 384
N_HEADS = 6
HEAD_PAIRS = N_HEADS // 2
DSA_SPAN = 128
DSA_BLOCK = 128
DSA_DILATIONS = (1, 4, 16)
RMS_EPS = 1e-6
NEG = -1e30
LOG2E = 1.4426950408889634
Q_SCALE = HEAD_DIM ** -0.5 * LOG2E
S5_READ_GROUPS = 4
PRE_GROUPS = 2
DSA_LOOKAHEAD = 3
FOX_LOOKAHEAD = 6

LANES = 128
BF16_SUBLANES = 16
VMEM_LIMIT = 56 * 1024 * 1024

NT_DIMS = (((1,), (1,)), ((), ()))


def _rms_scale(x, g):
    ms = jnp.mean(x * x, axis=-1, keepdims=True)
    return x * lax.rsqrt(ms + RMS_EPS) * g


def _params(*sem):
    return pltpu.CompilerParams(dimension_semantics=sem, vmem_limit_bytes=VMEM_LIMIT)


def _premix_kernel(x_ref, g_ref, wu_ref, wdk_ref, wqt_ref, wvt_ref, wft_ref,
                   u_ref, qd_ref, k_ref, qt_ref, vt_ref, ft_ref):
    tm = x_ref.shape[1]
    groups = [slice(i * tm // PRE_GROUPS, (i + 1) * tm // PRE_GROUPS) for i in range(PRE_GROUPS)]
    hs = [_rms_scale(x_ref[0, r, :], g_ref[...]).astype(BF16) for r in groups]
    for r, h in zip(groups, hs):
        u_ref[0, r, :] = jnp.dot(h, wu_ref[...], preferred_element_type=F32)
        dk = jnp.dot(h, wdk_ref[...], preferred_element_type=F32)
        qd_ref[0, r, :] = dk[:, :3 * ATT_WIDTH]
        k_ref[0, r, :] = dk[:, 3 * ATT_WIDTH:].astype(BF16)
        nt = lambda w_ref: lax.dot_general(w_ref[...], h, NT_DIMS, preferred_element_type=F32)
        qt_ref[0, :, r] = (nt(wqt_ref) * Q_SCALE).astype(BF16)
        vt_ref[0, :, r] = nt(wvt_ref).astype(BF16)
        ft_ref[0, :, r] = nt(wft_ref)


def _premix(x, g, wu, wdk, wqt, wvt, wft, *, tm):
    B, S, D = x.shape
    grid = (B, S // tm)
    const = lambda b, j: (0, 0)
    full = lambda a: pl.BlockSpec(a.shape, const)
    tok = lambda w: pl.BlockSpec((1, tm, w), lambda b, j: (b, j, 0))
    feat = lambda r: pl.BlockSpec((1, r, tm), lambda b, j: (b, 0, j))
    return pl.pallas_call(
        _premix_kernel,
        grid=grid,
        in_specs=[tok(D), full(g), full(wu), full(wdk), full(wqt), full(wvt), full(wft)],
        out_specs=[
            tok(SSM_WIDTH), tok(3 * ATT_WIDTH), tok(ATT_WIDTH), feat(ATT_WIDTH), feat(ATT_WIDTH),
            feat(8 * HEAD_PAIRS),
        ],
        out_shape=[
            jax.ShapeDtypeStruct((B, S, SSM_WIDTH), F32),
            jax.ShapeDtypeStruct((B, S, 3 * ATT_WIDTH), F32),
            jax.ShapeDtypeStruct((B, S, ATT_WIDTH), BF16),
            jax.ShapeDtypeStruct((B, ATT_WIDTH, S), BF16),
            jax.ShapeDtypeStruct((B, ATT_WIDTH, S), BF16),
            jax.ShapeDtypeStruct((B, 8 * HEAD_PAIRS, S), F32),
        ],
        compiler_params=_params("parallel", "parallel"),
        name="premix",
    )(x, g, wu, wdk, wqt, wvt, wft)


def _time_pitch(batch):
    assert batch % 8 == 0, batch
    return batch + 8 if (batch // 8) % 2 == 0 else batch


def _s5_kernel(u_ref, bt_ref, are_ref, aim_ref, ct_ref, d_ref, wg_ref, bg_ref, y_ref,
               tm_ref, xs_ref, h_ref, *, col_tile):
    batch, steps, width = u_ref.shape
    n_state = are_ref.shape[1]
    pitch = _time_pitch(batch)
    lane_tiles = [slice(t * LANES, (t + 1) * LANES) for t in range(width // LANES)]

    @pl.when(pl.program_id(0) == 0)
    def _():
        h_ref[...] = jnp.zeros_like(h_ref)

    for b in range(batch):
        for t, lanes in enumerate(lane_tiles):
            tm_ref[t, pl.ds(b, steps, stride=pitch), :] = u_ref[b, :, lanes]
    u = jnp.concatenate(
        [jnp.concatenate([tm_ref[t, s * pitch:s * pitch + batch, :] for s in range(steps)], axis=0)
         for t in range(len(lane_tiles))], axis=1)
    group_rows = steps // S5_READ_GROUPS * batch
    for gi in range(S5_READ_GROUPS):
        rows = slice(gi * group_rows, (gi + 1) * group_rows)
        xs_ref[rows, :] = jnp.dot(u[rows, :].astype(BF16), bt_ref[...], preferred_element_type=F32)

    for c in range(n_state // col_tile):
        re = slice(c * col_tile, (c + 1) * col_tile)
        im = slice(n_state + c * col_tile, n_state + (c + 1) * col_tile)
        ar = are_ref[:, re]
        ai = aim_ref[:, re]

        def step(s, carry, re=re, im=im, ar=ar, ai=ai):
            hr, hi = carry
            rows = pl.ds(pl.multiple_of(s * batch, batch), batch)
            nr = ar * hr - ai * hi + xs_ref[rows, re]
            ni = ar * hi + ai * hr + xs_ref[rows, im]
            xs_ref[rows, re] = nr
            xs_ref[rows, im] = ni
            return nr, ni

        hr, hi = lax.fori_loop(0, steps, step, (h_ref[:, re], h_ref[:, im]), unroll=2)
        h_ref[:, re] = hr
        h_ref[:, im] = hi

    group_steps = steps // S5_READ_GROUPS

    def project(gi):
        rows = slice(gi * group_steps * batch, (gi + 1) * group_steps * batch)
        y = jnp.dot(xs_ref[rows, :].astype(BF16), ct_ref[...], preferred_element_type=F32)
        return y + d_ref[...] * u[rows, :]

    def finish(gi, y):
        g = jax.nn.gelu(y)
        z = jnp.dot(g.astype(BF16), wg_ref[...], preferred_element_type=F32) + bg_ref[...]
        out = g * jax.nn.sigmoid(z)
        for t, lanes in enumerate(lane_tiles):
            for k in range(group_steps):
                s = gi * group_steps + k
                tm_ref[t, s * pitch:s * pitch + batch, :] = out[k * batch:(k + 1) * batch, lanes]

    y_next = project(0)
    for gi in range(S5_READ_GROUPS):
        y_cur = y_next
        if gi + 1 < S5_READ_GROUPS:
            y_next = project(gi + 1)
        finish(gi, y_cur)

    for b in range(batch):
        for t, lanes in enumerate(lane_tiles):
            y_ref[b, :, lanes] = tm_ref[t, pl.ds(b, steps, stride=pitch), :]


def _s5(u, bt, a_re, a_im, ct, d, wg, bg, *, steps):
    B, S, width = u.shape
    n_cols = bt.shape[1]
    const = lambda i: (0, 0)
    full = lambda a: pl.BlockSpec(a.shape, const)
    chunk = pl.BlockSpec((B, steps, width), lambda i: (0, i, 0))
    return pl.pallas_call(
        functools.partial(_s5_kernel, col_tile=2 * LANES),
        grid=(S // steps,),
        in_specs=[chunk, full(bt), full(a_re), full(a_im), full(ct), full(d), full(wg), full(bg)],
        out_specs=chunk,
        out_shape=jax.ShapeDtypeStruct((B, S, width), F32),
        scratch_shapes=[
            pltpu.VMEM((width // LANES, steps * _time_pitch(B), LANES), F32),
            pltpu.VMEM((steps * B, n_cols), F32),
            pltpu.VMEM((B, n_cols), F32),
        ],
        compiler_params=_params("arbitrary"),
        name="s5",
    )(u, bt, a_re, a_im, ct, d, wg, bg)


def _dsa_scores(qb, kw, valid, head0):
    out = []
    for h in range(2):
        qh = jnp.where(head0 if h == 0 else jnp.logical_not(head0), qb, jnp.zeros_like(qb))
        s = lax.dot_general(qh, kw, NT_DIMS, preferred_element_type=F32)
        out.append(jnp.where(valid, s, NEG))
    return out


def _dsa_finish(scores, vw, head0):
    vx = jnp.concatenate([vw, jnp.ones_like(vw)], axis=1)
    parts = []
    for s in scores:
        m = jnp.max(s, axis=-1, keepdims=True)
        r = jnp.dot(jnp.exp2(s - m).astype(BF16), vx, preferred_element_type=F32)
        parts.append((r[:, :LANES], jnp.broadcast_to(m, (s.shape[0], LANES)), r[:, LANES:]))
    return tuple(jnp.where(head0, a, b) for a, b in zip(*parts))


def _softmax_merge(n1, m1, l1, n2, m2, l2):
    m = jnp.maximum(m1, m2)
    e1 = jnp.exp2(m1 - m)
    e2 = jnp.exp2(m2 - m)
    return n1 * e1 + n2 * e2, m, l1 * e1 + l2 * e2


def _dsa_kernel(q_ref, k_ref, v_ref, o_ref, acc_ref, m_ref, l_ref,
                q4_ref, k4_ref, v4_ref, acc4_ref, m4_ref, l4_ref, *, seq):
    blk = DSA_BLOCK
    n_blocks = seq // blk
    head0 = lax.broadcasted_iota(jnp.int32, (blk, LANES), 1) < HEAD_DIM
    qi = lax.broadcasted_iota(jnp.int32, (blk, 2 * blk), 0)
    ki = lax.broadcasted_iota(jnp.int32, (blk, 2 * blk), 1)
    band = (ki >= qi) & (ki <= qi + DSA_SPAN)
    causal = (lax.broadcasted_iota(jnp.int32, (blk, blk), 1)
              <= lax.broadcasted_iota(jnp.int32, (blk, blk), 0))

    def bf(x, scale=None):
        return (x if scale is None else x * scale).astype(BF16)

    pending = []

    def retire():
        scores, vw, sink = pending.pop(0)
        sink(_dsa_finish(scores, vw, head0))

    def submit(q, kw, vw, sink):
        valid = band if kw.shape[0] == 2 * blk else causal
        pending.append((_dsa_scores(bf(q, Q_SCALE), bf(kw), valid, head0), bf(vw), sink))
        if len(pending) > DSA_LOOKAHEAD:
            retire()

    def drain():
        while pending:
            retire()

    def window(ref_rows, start, has_prev):
        return ref_rows(start - blk, 2 * blk) if has_prev else ref_rows(start, blk)

    nat = lambda ref: (lambda start, n: ref[0, start:start + n, :])
    seg = lambda ref: (lambda start, n: ref[start:start + n, :])
    state = (acc_ref, m_ref, l_ref)
    state4 = (acc4_ref, m4_ref, l4_ref)

    d_mid = DSA_DILATIONS[1]
    seg_len = seq // d_mid
    for rho in range(d_mid):
        for src, dst_ref in ((q_ref, q4_ref), (k_ref, k4_ref), (v_ref, v4_ref)):
            dst_ref[rho * seg_len:(rho + 1) * seg_len, :] = src[0, pl.ds(rho, seg_len, stride=d_mid), :]

    def init_state(rows):
        def sink(new):
            for ref, val in zip(state, new):
                ref[rows, :] = val
        return sink

    for n in range(n_blocks):
        start = n * blk
        submit(nat(q_ref)(start, blk), window(nat(k_ref), start, n > 0),
               window(nat(v_ref), start, n > 0), init_state(slice(start, start + blk)))
    drain()

    for rho in range(d_mid):
        for src, dst_ref in zip(state, state4):
            dst_ref[rho * seg_len:(rho + 1) * seg_len, :] = src[pl.ds(rho, seg_len, stride=d_mid), :]

    def merge_state(rows):
        def sink(new):
            old = tuple(ref[rows, :] for ref in state4)
            for ref, val in zip(state4, _softmax_merge(*old, *new)):
                ref[rows, :] = val
        return sink

    for rho in range(d_mid):
        for n in range(seg_len // blk):
            start = rho * seg_len + n * blk
            submit(seg(q4_ref)(start, blk), window(seg(k4_ref), start, n > 0),
                   window(seg(v4_ref), start, n > 0), merge_state(slice(start, start + blk)))
    drain()

    def finalize(rows):
        def sink(new):
            old = tuple(ref[rows, :] for ref in state4)
            num, _, den = _softmax_merge(*old, *new)
            acc4_ref[rows, :] = num / den
        return sink

    inner = DSA_DILATIONS[2] // d_mid
    for rho in range(d_mid):
        for c in range(inner):
            rs = pl.ds(rho * seg_len + c, blk, stride=inner)
            submit(q4_ref[rs, :], k4_ref[rs, :], v4_ref[rs, :], finalize(rs))
    drain()

    for rho in range(d_mid):
        o_ref[0, pl.ds(rho, seg_len, stride=d_mid), :] = acc4_ref[rho * seg_len:(rho + 1) * seg_len, :]


def _dsa(qkv):
    B, S, _ = qkv.shape
    assert S == DSA_DILATIONS[2] * DSA_BLOCK, S
    blockspec = lambda off: pl.BlockSpec((1, S, LANES), lambda b, p: (b, 0, off + p))
    return pl.pallas_call(
        functools.partial(_dsa_kernel, seq=S),
        grid=(B, HEAD_PAIRS),
        in_specs=[blockspec(0), blockspec(HEAD_PAIRS), blockspec(2 * HEAD_PAIRS)],
        out_specs=blockspec(0),
        out_shape=jax.ShapeDtypeStruct((B, S, ATT_WIDTH), F32),
        scratch_shapes=[pltpu.VMEM((S, LANES), F32)] * 9,
        compiler_params=_params("parallel", "parallel"),
        name="dsa",
    )(qkv, qkv, qkv)


def _bf16_pieces(x):
    hi = x.astype(BF16).astype(F32)
    r1 = x - hi
    lo = r1.astype(BF16).astype(F32)
    return hi, lo, (r1 - lo).astype(BF16).astype(F32)


def _lane_cumsum(x, block):
    n = x.shape[1]
    starts = range(0, n, block)
    pieces = _bf16_pieces(x)
    lhs = jnp.concatenate([p[:, a:a + block] for a in starts for p in pieces], axis=0)
    tri = (lax.broadcasted_iota(jnp.int32, (block, block), 0)
           <= lax.broadcasted_iota(jnp.int32, (block, block), 1)).astype(BF16)
    res = jnp.dot(lhs.astype(BF16), tri, preferred_element_type=F32)
    rows = x.shape[0]
    within = [res[3 * rows * j:3 * rows * j + rows] + res[3 * rows * j + rows:3 * rows * j + 2 * rows]
              + res[3 * rows * j + 2 * rows:3 * rows * (j + 1)] for j in range(len(starts))]
    out, offset = [], None
    for blk in within:
        out.append(blk if offset is None else blk + offset)
        total = blk[:, block - 1:block]
        offset = total if offset is None else offset + total
    return jnp.concatenate(out, axis=1)


def _fox_kernel(k_ref, qt_ref, vt_ref, ft_ref, bf_ref, o_ref, qa_ref, ka_ref, vx_ref, *, seq, tq):
    x = ft_ref[0] + bf_ref[...]
    logf = jnp.minimum(x, 0.0) - jnp.log1p(jnp.exp(-jnp.abs(x)))
    cum = _lane_cumsum(logf, 2 * LANES)

    hi, lo, lo2 = _bf16_pieces(cum * LOG2E)

    sub = lax.broadcasted_iota(jnp.int32, (8, seq), 0)
    rows8 = lambda v, r: jnp.broadcast_to(v[r:r + 1, :], (8, seq))
    pad = jnp.zeros((HEAD_DIM - 8, seq), F32)
    q_tiles, k_tiles = [], []
    for r in range(2):
        h8, l8, l28 = rows8(hi, r), rows8(lo, r), rows8(lo2, r)
        q_tiles += [jnp.where(sub == 0, h8, jnp.where(sub == 1, l8, jnp.where(sub == 2, l28,
                              jnp.where(sub < 6, 1.0, 0.0)))), pad]
        k_tiles += [jnp.where(sub < 3, 1.0, jnp.where(sub == 3, -h8, jnp.where(sub == 4, -l8,
                              jnp.where(sub == 5, -l28, 0.0)))), pad]
    aq_t = jnp.concatenate(q_tiles, axis=0).astype(BF16)
    ak = jnp.concatenate(k_tiles, axis=0).T

    lane = lax.broadcasted_iota(jnp.int32, (1, LANES), 1)
    qt = qt_ref[0]
    kv = k_ref[0]
    zeros_t = jnp.zeros((HEAD_DIM, seq), BF16)
    for h in range(2):
        in_head = (lane >= HEAD_DIM * h) & (lane < HEAD_DIM * (h + 1))
        qa_ref[h, :LANES, :] = (jnp.concatenate([qt[:HEAD_DIM], zeros_t], axis=0) if h == 0 else
                                jnp.concatenate([zeros_t, qt[HEAD_DIM:]], axis=0))
        qa_ref[h, LANES:, :] = aq_t
        ka_ref[h, :, :LANES] = kv
        ka_ref[h, :, LANES:] = (ak * in_head.astype(F32)).astype(BF16)

    for h in range(2):
        vx_ref[h, :HEAD_DIM, :] = vt_ref[0, HEAD_DIM * h:HEAD_DIM * (h + 1), :]
        vx_ref[h, HEAD_DIM:, :] = jnp.ones((BF16_SUBLANES, seq), BF16)

    key_le_query = (lax.broadcasted_iota(jnp.int32, (tq, tq), 0)
                    <= lax.broadcasted_iota(jnp.int32, (tq, tq), 1))

    n_q = seq // tq
    assert n_q % 2 == 0, n_q
    steps = []
    for t in range(n_q + 1):
        for a in range(n_q // 2):
            j, iq = (t, a) if t <= a else (t - a - 1, n_q - 1 - a)
            steps += [(j, iq, 0), (j, iq, 1)]
    state = {}
    outs = {}

    def scores(j, iq, h):
        s = jnp.dot(ka_ref[h, j * tq:(j + 1) * tq, :], qa_ref[h, :, iq * tq:(iq + 1) * tq],
                    preferred_element_type=F32)
        return jnp.where(key_le_query, s, NEG) if j == iq else s

    def absorb(j, iq, h, s):
        m = jnp.max(s, axis=0, keepdims=True)
        if j > 0:
            m_old, acc_old = state[iq, h]
            m = jnp.maximum(m_old, m)
        acc = jnp.dot(vx_ref[h, :, j * tq:(j + 1) * tq], jnp.exp2(s - m).astype(BF16),
                      preferred_element_type=F32)
        if j > 0:
            acc = acc_old * jnp.exp2(m_old - m) + acc
        state[iq, h] = (m, acc)
        if j == iq:
            del state[iq, h]
            outs[iq, h] = acc[:HEAD_DIM, :] / acc[HEAD_DIM:HEAD_DIM + 1, :]
            if h == 1:
                o_ref[0, iq * tq:(iq + 1) * tq, :] = jnp.concatenate(
                    [outs.pop((iq, 0)), outs.pop((iq, 1))], axis=0).T

    pending = []
    for step in steps:
        pending.append((step, scores(*step)))
        if len(pending) > FOX_LOOKAHEAD:
            done, s = pending.pop(0)
            absorb(*done, s)
    for done, s in pending:
        absorb(*done, s)


def _fox(k, qt, vt, ft, bf, *, tq):
    B, S, _ = k.shape
    tok = pl.BlockSpec((1, S, LANES), lambda b, p: (b, 0, p))
    feat = lambda rows: pl.BlockSpec((1, rows, S), lambda b, p: (b, p, 0))
    return pl.pallas_call(
        functools.partial(_fox_kernel, seq=S, tq=tq),
        grid=(B, HEAD_PAIRS),
        in_specs=[tok, feat(LANES), feat(LANES), feat(8),
                  pl.BlockSpec((8, S), lambda b, p: (p, 0))],
        out_specs=tok,
        out_shape=jax.ShapeDtypeStruct((B, S, ATT_WIDTH), F32),
        scratch_shapes=[pltpu.VMEM((2, 2 * LANES, S), BF16),
                        pltpu.VMEM((2, S, 2 * LANES), BF16),
                        pltpu.VMEM((2, HEAD_DIM + BF16_SUBLANES, S), BF16)],
        compiler_params=_params("parallel", "parallel"),
        name="fox",
    )(k, qt, vt, ft, bf)


MXU_WIDTH = 256
MIX_GROUPS = 2


def _hidden_chunks(d_ff, target):
    assert d_ff % MXU_WIDTH == 0, d_ff
    tiles = d_ff // MXU_WIDTH
    n = -(-d_ff // target)
    sizes = [(tiles // n + (i < tiles % n)) * MXU_WIDTH for i in range(n)]
    starts = [sum(sizes[:i]) for i in range(n)]
    return [slice(a, a + w) for a, w in zip(starts, sizes)]


def _mixffn_kernel(x_ref, ys_ref, yd_ref, yf_ref, gs_ref, gd_ref, gf_ref, wo_ref,
                   gp_ref, g1_ref, wg_ref, wu_ref, wdn_ref, g2_ref, o_ref, *, f_target):
    tm = x_ref.shape[1]
    groups = [slice(i * tm // MIX_GROUPS, (i + 1) * tm // MIX_GROUPS) for i in range(MIX_GROUPS)]
    dot = lambda a, b: jnp.dot(a, b, preferred_element_type=F32)

    def mix_in(r):
        mixed = jnp.concatenate([_rms_scale(ys_ref[0, r, :], gs_ref[...]).astype(BF16),
                                 _rms_scale(yd_ref[0, r, :], gd_ref[...]).astype(BF16),
                                 _rms_scale(yf_ref[0, r, :], gf_ref[...]).astype(BF16)], axis=1)
        return dot(mixed, wo_ref[...])

    o = [mix_in(r) for r in groups]
    x, h, acc = [], [], []
    chunks = _hidden_chunks(wg_ref.shape[1], f_target)
    gate_up = lambda hg, cols: (dot(hg, wg_ref[:, cols]), dot(hg, wu_ref[:, cols]))
    gu = []
    for g, r in enumerate(groups):
        x.append(x_ref[0, r, :] + _rms_scale(o[g], gp_ref[...]))
        h.append(_rms_scale(x[g], g1_ref[...]).astype(BF16))
        gu.append(gate_up(h[g], chunks[0]))
    for c, cols in enumerate(chunks):
        nxt = []
        for g in range(MIX_GROUPS):
            gate, up = gu[g]
            act = (gate * jax.nn.sigmoid(gate) * up).astype(BF16)
            if c + 1 < len(chunks):
                nxt.append(gate_up(h[g], chunks[c + 1]))
            down = dot(act, wdn_ref[cols, :])
            if c == 0:
                acc.append(down)
            else:
                acc[g] = acc[g] + down
        gu = nxt
    for g, r in enumerate(groups):
        o_ref[0, r, :] = x[g] + _rms_scale(acc[g], g2_ref[...])


def _mixffn(x, ys_sb, yd, yf, gs, gd, gf, wo, gp, g1, wg, wu, wdn, g2, *, tm, f_target):
    B, S, D = x.shape
    const = lambda b, j: (0, 0)
    tok = lambda w: pl.BlockSpec((1, tm, w), lambda b, j: (b, j, 0))
    full = lambda a: pl.BlockSpec(a.shape, const)
    weights = (gs, gd, gf, wo, gp, g1, wg, wu, wdn, g2)
    return pl.pallas_call(
        functools.partial(_mixffn_kernel, f_target=f_target),
        grid=(B, S // tm),
        in_specs=[tok(D), tok(SSM_WIDTH), tok(ATT_WIDTH),
                  tok(ATT_WIDTH)] + [full(a) for a in weights],
        out_specs=tok(D),
        out_shape=jax.ShapeDtypeStruct((B, S, D), F32),
        compiler_params=_params("parallel", "parallel"),
        name="mixffn",
    )(x, ys_sb, yd, yf, *weights)


def _s5_matrices(log_dt, a_re, a_im, b_re, b_im, c_re, c_im):
    G, P = a_re.shape
    C = b_re.shape[-1]
    dt = jnp.exp(log_dt)[:, None]
    decay = jnp.exp(a_re * dt)
    abar_re = decay * jnp.cos(a_im * dt)
    abar_im = decay * jnp.sin(a_im * dt)
    mag2 = a_re * a_re + a_im * a_im
    coef_re = ((abar_re - 1.0) * a_re + abar_im * a_im) / mag2
    coef_im = (abar_im * a_re - (abar_re - 1.0) * a_im) / mag2
    bbar_re = coef_re[..., None] * b_re - coef_im[..., None] * b_im
    bbar_im = coef_re[..., None] * b_im + coef_im[..., None] * b_re
    eye = jnp.eye(G, dtype=F32)
    bt_re = jnp.einsum('gpc,gh->gchp', bbar_re, eye).reshape(G * C, G * P)
    bt_im = jnp.einsum('gpc,gh->gchp', bbar_im, eye).reshape(G * C, G * P)
    bt = jnp.concatenate([bt_re, bt_im], axis=1).astype(BF16)
    ct_re = jnp.einsum('gcp,gh->gphc', c_re, eye).reshape(G * P, G * C)
    ct_im = jnp.einsum('gcp,gh->gphc', -c_im, eye).reshape(G * P, G * C)
    ct = jnp.concatenate([ct_re, ct_im], axis=0).astype(BF16)
    return bt, abar_re.reshape(1, G * P), abar_im.reshape(1, G * P), ct


def _per_pair_rows(v):
    out = jnp.zeros((HEAD_PAIRS, 8) + v.shape[1:], v.dtype)
    out = out.at[:, :2].set(v.reshape((HEAD_PAIRS, 2) + v.shape[1:]))
    return out.reshape((HEAD_PAIRS * 8,) + v.shape[1:])


def kernel(x, g_pre_mix, w_in, b_f, ssm_log_dt, ssm_a_re, ssm_a_im, ssm_b_re, ssm_b_im, ssm_c_re, ssm_c_im, ssm_d, w_glu, b_glu, g_mix, w_out, g_post_mix, g_pre_ffn, w_gate_up, w_down, g_post_ffn):
    B, S, D = x.shape
    depth = w_in.shape[0]
    d_ff = w_down.shape[1]
    tm = 512
    o1 = SSM_WIDTH
    o2 = o1 + 3 * ATT_WIDTH
    o3 = o2 + 2 * ATT_WIDTH
    o4 = o3 + ATT_WIDTH
    row = lambda v: v.reshape(1, -1)

    for l in range(depth):
        w = w_in[l]
        u_ssm, qkv_dsa, k_fox, qt_fox, vt_fox, ft = _premix(
            x, row(g_pre_mix[l]), w[:, :o1].astype(BF16),
            jnp.concatenate([w[:, o1:o2], w[:, o2 + ATT_WIDTH:o3]], axis=1).astype(BF16),
            w[:, o2:o2 + ATT_WIDTH].T.astype(BF16), w[:, o3:o4].T.astype(BF16),
            _per_pair_rows(w[:, o4:].T).astype(BF16), tm=2 * tm)

        bt, a_re, a_im, ct = _s5_matrices(ssm_log_dt[l], ssm_a_re[l], ssm_a_im[l], ssm_b_re[l],
                                          ssm_b_im[l], ssm_c_re[l], ssm_c_im[l])
        y_ssm = _s5(u_ssm, bt, a_re, a_im, ct, row(ssm_d[l]), w_glu[l].astype(BF16),
                    row(b_glu[l]), steps=64)

        y_dsa = _dsa(qkv_dsa)

        bf = jnp.broadcast_to(_per_pair_rows(b_f[l][:, None]), (8 * HEAD_PAIRS, S))
        y_fox = _fox(k_fox, qt_fox, vt_fox, ft, bf, tq=256)

        gm = g_mix[l]
        wo = w_out[l].astype(BF16)
        wgu = w_gate_up[l].astype(BF16)
        x = _mixffn(x, y_ssm, y_dsa, y_fox,
                    row(gm[:o1]), row(gm[o1:o1 + ATT_WIDTH]), row(gm[o1 + ATT_WIDTH:]),
                    wo, row(g_post_mix[l]),
                    row(g_pre_ffn[l]), wgu[:, :d_ff], wgu[:, d_ff:], w_down[l].astype(BF16),
                    row(g_post_ffn[l]), tm=tm, f_target=1536)
    return x
```

```python
import functools

import jax
import jax.numpy as jnp
from jax import lax
from jax.experimental import pallas as pl
from jax.experimental.pallas import tpu as pltpu

F32 = jnp.float32
BF16 = jnp.bfloat16

HEAD_DIM = 64
SSM_WIDTH = 256
ATT_WIDTH = 384
N_HEADS = 6
HEAD_PAIRS = N_HEADS // 2
DSA_SPAN = 128
DSA_BLOCK = 128
DSA_DILATIONS = (1, 4, 16)
RMS_EPS = 1e-6
NEG = -1e30
LOG2E = 1.4426950408889634
Q_SCALE = HEAD_DIM ** -0.5 * LOG2E
PREMIX_TILE = 1024
MIX_TILE = 512
FFN_CHUNK = 1536
S5_STEPS = 64
FOX_BLOCK = 256
S5_READ_GROUPS = 4
PRE_GROUPS = 2
DSA_LOOKAHEAD = 3
FOX_LOOKAHEAD = 6

LANES = 128
BF16_SUBLANES = 16
VMEM_LIMIT = 56 * 1024 * 1024

NT_DIMS = (((1,), (1,)), ((), ()))


def _rms_scale(x, g):
    ms = jnp.mean(x * x, axis=-1, keepdims=True)
    return x * lax.rsqrt(ms + RMS_EPS) * g


def _params(*sem):
    return pltpu.CompilerParams(dimension_semantics=sem, vmem_limit_bytes=VMEM_LIMIT)


def _premix_kernel(x_ref, g_ref, wu_ref, wdk_ref, wqt_ref, wvt_ref, wft_ref,
                   u_ref, qd_ref, k_ref, qt_ref, vt_ref, ft_ref):
    tm = x_ref.shape[1]
    groups = [slice(i * tm // PRE_GROUPS, (i + 1) * tm // PRE_GROUPS) for i in range(PRE_GROUPS)]
    hs = [_rms_scale(x_ref[0, r, :], g_ref[...]).astype(BF16) for r in groups]
    for r, h in zip(groups, hs):
        u_ref[0, r, :] = jnp.dot(h, wu_ref[...], preferred_element_type=F32)
        dk = jnp.dot(h, wdk_ref[...], preferred_element_type=F32)
        qd_ref[0, r, :] = dk[:, :3 * ATT_WIDTH]
        k_ref[0, r, :] = dk[:, 3 * ATT_WIDTH:].astype(BF16)
        nt = lambda w_ref: lax.dot_general(w_ref[...], h, NT_DIMS, preferred_element_type=F32)
        qt_ref[0, :, r] = (nt(wqt_ref) * Q_SCALE).astype(BF16)
        vt_ref[0, :, r] = nt(wvt_ref).astype(BF16)
        ft_ref[0, :, r] = nt(wft_ref)


def _premix(x, g, wu, wdk, wqt, wvt, wft, *, tm):
    B, S, D = x.shape
    grid = (B, S // tm)
    const = lambda b, j: (0, 0)
    full = lambda a: pl.BlockSpec(a.shape, const)
    tok = lambda w: pl.BlockSpec((1, tm, w), lambda b, j: (b, j, 0))
    feat = lambda r: pl.BlockSpec((1, r, tm), lambda b, j: (b, 0, j))
    return pl.pallas_call(
        _premix_kernel,
        grid=grid,
        in_specs=[tok(D), full(g), full(wu), full(wdk), full(wqt), full(wvt), full(wft)],
        out_specs=[
            tok(SSM_WIDTH), tok(3 * ATT_WIDTH), tok(ATT_WIDTH), feat(ATT_WIDTH), feat(ATT_WIDTH),
            feat(8 * HEAD_PAIRS),
        ],
        out_shape=[
            jax.ShapeDtypeStruct((B, S, SSM_WIDTH), F32),
            jax.ShapeDtypeStruct((B, S, 3 * ATT_WIDTH), F32),
            jax.ShapeDtypeStruct((B, S, ATT_WIDTH), BF16),
            jax.ShapeDtypeStruct((B, ATT_WIDTH, S), BF16),
            jax.ShapeDtypeStruct((B, ATT_WIDTH, S), BF16),
            jax.ShapeDtypeStruct((B, 8 * HEAD_PAIRS, S), F32),
        ],
        compiler_params=_params("parallel", "parallel"),
        name="premix",
    )(x, g, wu, wdk, wqt, wvt, wft)


def _time_pitch(batch):
    assert batch % 8 == 0, batch
    return batch + 8 if (batch // 8) % 2 == 0 else batch


def _s5_kernel(u_ref, bt_ref, are_ref, aim_ref, ct_ref, d_ref, wg_ref, bg_ref, y_ref,
               tm_ref, xs_ref, h_ref, *, col_tile):
    batch, steps, width = u_ref.shape
    n_state = are_ref.shape[1]
    pitch = _time_pitch(batch)
    lane_tiles = [slice(t * LANES, (t + 1) * LANES) for t in range(width // LANES)]

    @pl.when(pl.program_id(0) == 0)
    def _():
        h_ref[...] = jnp.zeros_like(h_ref)

    for b in range(batch):
        for t, lanes in enumerate(lane_tiles):
            tm_ref[t, pl.ds(b, steps, stride=pitch), :] = u_ref[b, :, lanes]
    u = jnp.concatenate(
        [jnp.concatenate([tm_ref[t, s * pitch:s * pitch + batch, :] for s in range(steps)], axis=0)
         for t in range(len(lane_tiles))], axis=1)
    group_rows = steps // S5_READ_GROUPS * batch
    for gi in range(S5_READ_GROUPS):
        rows = slice(gi * group_rows, (gi + 1) * group_rows)
        xs_ref[rows, :] = jnp.dot(u[rows, :].astype(BF16), bt_ref[...], preferred_element_type=F32)

    for c in range(n_state // col_tile):
        re = slice(c * col_tile, (c + 1) * col_tile)
        im = slice(n_state + c * col_tile, n_state + (c + 1) * col_tile)
        ar = are_ref[:, re]
        ai = aim_ref[:, re]

        def step(s, carry, re=re, im=im, ar=ar, ai=ai):
            hr, hi = carry
            rows = pl.ds(pl.multiple_of(s * batch, batch), batch)
            nr = ar * hr - ai * hi + xs_ref[rows, re]
            ni = ar * hi + ai * hr + xs_ref[rows, im]
            xs_ref[rows, re] = nr
            xs_ref[rows, im] = ni
            return nr, ni

        hr, hi = lax.fori_loop(0, steps, step, (h_ref[:, re], h_ref[:, im]), unroll=2)
        h_ref[:, re] = hr
        h_ref[:, im] = hi

    group_steps = steps // S5_READ_GROUPS

    def project(gi):
        rows = slice(gi * group_steps * batch, (gi + 1) * group_steps * batch)
        y = jnp.dot(xs_ref[rows, :].astype(BF16), ct_ref[...], preferred_element_type=F32)
        return y + d_ref[...] * u[rows, :]

    def finish(gi, y):
        g = jax.nn.gelu(y)
        z = jnp.dot(g.astype(BF16), wg_ref[...], preferred_element_type=F32) + bg_ref[...]
        out = g * jax.nn.sigmoid(z)
        for t, lanes in enumerate(lane_tiles):
            for k in range(group_steps):
                s = gi * group_steps + k
                tm_ref[t, s * pitch:s * pitch + batch, :] = out[k * batch:(k + 1) * batch, lanes]

    y_next = project(0)
    for gi in range(S5_READ_GROUPS):
        y_cur = y_next
        if gi + 1 < S5_READ_GROUPS:
            y_next = project(gi + 1)
        finish(gi, y_cur)

    for b in range(batch):
        for t, lanes in enumerate(lane_tiles):
            y_ref[b, :, lanes] = tm_ref[t, pl.ds(b, steps, stride=pitch), :]


def _s5(u, bt, a_re, a_im, ct, d, wg, bg, *, steps):
    B, S, width = u.shape
    n_cols = bt.shape[1]
    const = lambda i: (0, 0)
    full = lambda a: pl.BlockSpec(a.shape, const)
    chunk = pl.BlockSpec((B, steps, width), lambda i: (0, i, 0))
    return pl.pallas_call(
        functools.partial(_s5_kernel, col_tile=2 * LANES),
        grid=(S // steps,),
        in_specs=[chunk, full(bt), full(a_re), full(a_im), full(ct), full(d), full(wg), full(bg)],
        out_specs=chunk,
        out_shape=jax.ShapeDtypeStruct((B, S, width), F32),
        scratch_shapes=[
            pltpu.VMEM((width // LANES, steps * _time_pitch(B), LANES), F32),
            pltpu.VMEM((steps * B, n_cols), F32),
            pltpu.VMEM((B, n_cols), F32),
        ],
        compiler_params=_params("arbitrary"),
        name="s5",
    )(u, bt, a_re, a_im, ct, d, wg, bg)


def _dsa_scores(qb, kw, valid, head0):
    out = []
    for h in range(2):
        qh = jnp.where(head0 if h == 0 else jnp.logical_not(head0), qb, jnp.zeros_like(qb))
        s = lax.dot_general(qh, kw, NT_DIMS, preferred_element_type=F32)
        out.append(jnp.where(valid, s, NEG))
    return out


def _dsa_finish(scores, vw, head0):
    vx = jnp.concatenate([vw, jnp.ones_like(vw)], axis=1)
    parts = []
    for s in scores:
        m = jnp.max(s, axis=-1, keepdims=True)
        r = jnp.dot(jnp.exp2(s - m).astype(BF16), vx, preferred_element_type=F32)
        parts.append((r[:, :LANES], jnp.broadcast_to(m, (s.shape[0], LANES)), r[:, LANES:]))
    return tuple(jnp.where(head0, a, b) for a, b in zip(*parts))


def _softmax_merge(n1, m1, l1, n2, m2, l2):
    m = jnp.maximum(m1, m2)
    e1 = jnp.exp2(m1 - m)
    e2 = jnp.exp2(m2 - m)
    return n1 * e1 + n2 * e2, m, l1 * e1 + l2 * e2


def _dsa_kernel(q_ref, k_ref, v_ref, o_ref, acc_ref, m_ref, l_ref,
                q4_ref, k4_ref, v4_ref, acc4_ref, m4_ref, l4_ref, *, seq):
    blk = DSA_BLOCK
    n_blocks = seq // blk
    head0 = lax.broadcasted_iota(jnp.int32, (blk, LANES), 1) < HEAD_DIM
    qi = lax.broadcasted_iota(jnp.int32, (blk, 2 * blk), 0)
    ki = lax.broadcasted_iota(jnp.int32, (blk, 2 * blk), 1)
    band = (ki >= qi) & (ki <= qi + DSA_SPAN)
    causal = (lax.broadcasted_iota(jnp.int32, (blk, blk), 1)
              <= lax.broadcasted_iota(jnp.int32, (blk, blk), 0))

    def bf(x, scale=None):
        return (x if scale is None else x * scale).astype(BF16)

    pending = []

    def retire():
        scores, vw, sink = pending.pop(0)
        sink(_dsa_finish(scores, vw, head0))

    def submit(q, kw, vw, sink):
        valid = band if kw.shape[0] == 2 * blk else causal
        pending.append((_dsa_scores(bf(q, Q_SCALE), bf(kw), valid, head0), bf(vw), sink))
        if len(pending) > DSA_LOOKAHEAD:
            retire()

    def drain():
        while pending:
            retire()

    def window(ref_rows, start, has_prev):
        return ref_rows(start - blk, 2 * blk) if has_prev else ref_rows(start, blk)

    nat = lambda ref: (lambda start, n: ref[0, start:start + n, :])
    seg = lambda ref: (lambda start, n: ref[start:start + n, :])
    state = (acc_ref, m_ref, l_ref)
    state4 = (acc4_ref, m4_ref, l4_ref)

    d_mid = DSA_DILATIONS[1]
    seg_len = seq // d_mid
    for rho in range(d_mid):
        for src, dst_ref in ((q_ref, q4_ref), (k_ref, k4_ref), (v_ref, v4_ref)):
            dst_ref[rho * seg_len:(rho + 1) * seg_len, :] = src[0, pl.ds(rho, seg_len, stride=d_mid), :]

    def init_state(rows):
        def sink(new):
            for ref, val in zip(state, new):
                ref[rows, :] = val
        return sink

    for n in range(n_blocks):
        start = n * blk
        submit(nat(q_ref)(start, blk), window(nat(k_ref), start, n > 0),
               window(nat(v_ref), start, n > 0), init_state(slice(start, start + blk)))
    drain()

    for rho in range(d_mid):
        for src, dst_ref in zip(state, state4):
            dst_ref[rho * seg_len:(rho + 1) * seg_len, :] = src[pl.ds(rho, seg_len, stride=d_mid), :]

    def merge_state(rows):
        def sink(new):
            old = tuple(ref[rows, :] for ref in state4)
            for ref, val in zip(state4, _softmax_merge(*old, *new)):
                ref[rows, :] = val
        return sink

    for rho in range(d_mid):
        for n in range(seg_len // blk):
            start = rho * seg_len + n * blk
            submit(seg(q4_ref)(start, blk), window(seg(k4_ref), start, n > 0),
                   window(seg(v4_ref), start, n > 0), merge_state(slice(start, start + blk)))
    drain()

    def finalize(rows):
        def sink(new):
            old = tuple(ref[rows, :] for ref in state4)
            num, _, den = _softmax_merge(*old, *new)
            acc4_ref[rows, :] = num / den
        return sink

    inner = DSA_DILATIONS[2] // d_mid
    for rho in range(d_mid):
        for c in range(inner):
            rs = pl.ds(rho * seg_len + c, blk, stride=inner)
            submit(q4_ref[rs, :], k4_ref[rs, :], v4_ref[rs, :], finalize(rs))
    drain()

    for rho in range(d_mid):
        o_ref[0, pl.ds(rho, seg_len, stride=d_mid), :] = acc4_ref[rho * seg_len:(rho + 1) * seg_len, :]


def _dsa(qkv):
    B, S, _ = qkv.shape
    assert S == DSA_DILATIONS[2] * DSA_BLOCK, S
    blockspec = lambda off: pl.BlockSpec((1, S, LANES), lambda b, p: (b, 0, off + p))
    return pl.pallas_call(
        functools.partial(_dsa_kernel, seq=S),
        grid=(B, HEAD_PAIRS),
        in_specs=[blockspec(0), blockspec(HEAD_PAIRS), blockspec(2 * HEAD_PAIRS)],
        out_specs=blockspec(0),
        out_shape=jax.ShapeDtypeStruct((B, S, ATT_WIDTH), F32),
        scratch_shapes=[pltpu.VMEM((S, LANES), F32)] * 9,
        compiler_params=_params("parallel", "parallel"),
        name="dsa",
    )(qkv, qkv, qkv)


def _bf16_pieces(x):
    hi = x.astype(BF16).astype(F32)
    r1 = x - hi
    lo = r1.astype(BF16).astype(F32)
    return hi, lo, (r1 - lo).astype(BF16).astype(F32)


def _lane_cumsum(x, block):
    n = x.shape[1]
    starts = range(0, n, block)
    pieces = _bf16_pieces(x)
    lhs = jnp.concatenate([p[:, a:a + block] for a in starts for p in pieces], axis=0)
    tri = (lax.broadcasted_iota(jnp.int32, (block, block), 0)
           <= lax.broadcasted_iota(jnp.int32, (block, block), 1)).astype(BF16)
    res = jnp.dot(lhs.astype(BF16), tri, preferred_element_type=F32)
    rows = x.shape[0]
    within = [res[3 * rows * j:3 * rows * j + rows] + res[3 * rows * j + rows:3 * rows * j + 2 * rows]
              + res[3 * rows * j + 2 * rows:3 * rows * (j + 1)] for j in range(len(starts))]
    out, offset = [], None
    for blk in within:
        out.append(blk if offset is None else blk + offset)
        total = blk[:, block - 1:block]
        offset = total if offset is None else offset + total
    return jnp.concatenate(out, axis=1)


def _fox_kernel(k_ref, qt_ref, vt_ref, ft_ref, bf_ref, o_ref, qa_ref, ka_ref, vx_ref, *, seq, tq):
    x = ft_ref[0] + bf_ref[...]
    logf = jnp.minimum(x, 0.0) - jnp.log1p(jnp.exp(-jnp.abs(x)))
    cum = _lane_cumsum(logf, 2 * LANES)

    hi, lo, lo2 = _bf16_pieces(cum * LOG2E)

    sub = lax.broadcasted_iota(jnp.int32, (8, seq), 0)
    rows8 = lambda v, r: jnp.broadcast_to(v[r:r + 1, :], (8, seq))
    pad = jnp.zeros((HEAD_DIM - 8, seq), F32)
    q_tiles, k_tiles = [], []
    for r in range(2):
        h8, l8, l28 = rows8(hi, r), rows8(lo, r), rows8(lo2, r)
        q_tiles += [jnp.where(sub == 0, h8, jnp.where(sub == 1, l8, jnp.where(sub == 2, l28,
                              jnp.where(sub < 6, 1.0, 0.0)))), pad]
        k_tiles += [jnp.where(sub < 3, 1.0, jnp.where(sub == 3, -h8, jnp.where(sub == 4, -l8,
                              jnp.where(sub == 5, -l28, 0.0)))), pad]
    aq_t = jnp.concatenate(q_tiles, axis=0).astype(BF16)
    ak = jnp.concatenate(k_tiles, axis=0).T

    lane = lax.broadcasted_iota(jnp.int32, (1, LANES), 1)
    qt = qt_ref[0]
    kv = k_ref[0]
    zeros_t = jnp.zeros((HEAD_DIM, seq), BF16)
    for h in range(2):
        in_head = (lane >= HEAD_DIM * h) & (lane < HEAD_DIM * (h + 1))
        qa_ref[h, :LANES, :] = (jnp.concatenate([qt[:HEAD_DIM], zeros_t], axis=0) if h == 0 else
                                jnp.concatenate([zeros_t, qt[HEAD_DIM:]], axis=0))
        qa_ref[h, LANES:, :] = aq_t
        ka_ref[h, :, :LANES] = kv
        ka_ref[h, :, LANES:] = (ak * in_head.astype(F32)).astype(BF16)

    for h in range(2):
        vx_ref[h, :HEAD_DIM, :] = vt_ref[0, HEAD_DIM * h:HEAD_DIM * (h + 1), :]
        vx_ref[h, HEAD_DIM:, :] = jnp.ones((BF16_SUBLANES, seq), BF16)

    key_le_query = (lax.broadcasted_iota(jnp.int32, (tq, tq), 0)
                    <= lax.broadcasted_iota(jnp.int32, (tq, tq), 1))

    n_q = seq // tq
    assert n_q % 2 == 0, n_q
    steps = []
    for t in range(n_q + 1):
        for a in range(n_q // 2):
            j, iq = (t, a) if t <= a else (t - a - 1, n_q - 1 - a)
            steps += [(j, iq, 0), (j, iq, 1)]
    state = {}
    outs = {}

    def scores(j, iq, h):
        s = jnp.dot(ka_ref[h, j * tq:(j + 1) * tq, :], qa_ref[h, :, iq * tq:(iq + 1) * tq],
                    preferred_element_type=F32)
        return jnp.where(key_le_query, s, NEG) if j == iq else s

    def absorb(j, iq, h, s):
        m = jnp.max(s, axis=0, keepdims=True)
        if j > 0:
            m_old, acc_old = state[iq, h]
            m = jnp.maximum(m_old, m)
        acc = jnp.dot(vx_ref[h, :, j * tq:(j + 1) * tq], jnp.exp2(s - m).astype(BF16),
                      preferred_element_type=F32)
        if j > 0:
            acc = acc_old * jnp.exp2(m_old - m) + acc
        state[iq, h] = (m, acc)
        if j == iq:
            del state[iq, h]
            outs[iq, h] = acc[:HEAD_DIM, :] / acc[HEAD_DIM:HEAD_DIM + 1, :]
            if h == 1:
                o_ref[0, iq * tq:(iq + 1) * tq, :] = jnp.concatenate(
                    [outs.pop((iq, 0)), outs.pop((iq, 1))], axis=0).T

    pending = []
    for step in steps:
        pending.append((step, scores(*step)))
        if len(pending) > FOX_LOOKAHEAD:
            done, s = pending.pop(0)
            absorb(*done, s)
    for done, s in pending:
        absorb(*done, s)


def _fox(k, qt, vt, ft, bf, *, tq):
    B, S, _ = k.shape
    tok = pl.BlockSpec((1, S, LANES), lambda b, p: (b, 0, p))
    feat = lambda rows: pl.BlockSpec((1, rows, S), lambda b, p: (b, p, 0))
    return pl.pallas_call(
        functools.partial(_fox_kernel, seq=S, tq=tq),
        grid=(B, HEAD_PAIRS),
        in_specs=[tok, feat(LANES), feat(LANES), feat(8),
                  pl.BlockSpec((8, S), lambda b, p: (p, 0))],
        out_specs=tok,
        out_shape=jax.ShapeDtypeStruct((B, S, ATT_WIDTH), F32),
        scratch_shapes=[pltpu.VMEM((2, 2 * LANES, S), BF16),
                        pltpu.VMEM((2, S, 2 * LANES), BF16),
                        pltpu.VMEM((2, HEAD_DIM + BF16_SUBLANES, S), BF16)],
        compiler_params=_params("parallel", "parallel"),
        name="fox",
    )(k, qt, vt, ft, bf)


MXU_WIDTH = 256
MIX_GROUPS = 2


def _hidden_chunks(d_ff, target):
    assert d_ff % MXU_WIDTH == 0, d_ff
    tiles = d_ff // MXU_WIDTH
    n = -(-d_ff // target)
    sizes = [(tiles // n + (i < tiles % n)) * MXU_WIDTH for i in range(n)]
    starts = [sum(sizes[:i]) for i in range(n)]
    return [slice(a, a + w) for a, w in zip(starts, sizes)]


def _mixffn_kernel(x_ref, ys_ref, yd_ref, yf_ref, gs_ref, gd_ref, gf_ref, wo_ref,
                   gp_ref, g1_ref, wgu_ref, wdn_ref, g2_ref, o_ref, *, f_target):
    tm = x_ref.shape[1]
    groups = [slice(i * tm // MIX_GROUPS, (i + 1) * tm // MIX_GROUPS) for i in range(MIX_GROUPS)]
    dot = lambda a, b: jnp.dot(a, b, preferred_element_type=F32)

    def mix_in(r):
        mixed = jnp.concatenate([_rms_scale(ys_ref[0, r, :], gs_ref[...]).astype(BF16),
                                 _rms_scale(yd_ref[0, r, :], gd_ref[...]).astype(BF16),
                                 _rms_scale(yf_ref[0, r, :], gf_ref[...]).astype(BF16)], axis=1)
        return dot(mixed, wo_ref[...])

    o = [mix_in(r) for r in groups]
    x, h, acc = [], [], []
    d_ff = wdn_ref.shape[0]
    chunks = _hidden_chunks(d_ff, f_target)
    gate_up = lambda hg, cols: (dot(hg, wgu_ref[:, cols]),
                                dot(hg, wgu_ref[:, d_ff + cols.start:d_ff + cols.stop]))
    gu = []
    for g, r in enumerate(groups):
        x.append(x_ref[0, r, :] + _rms_scale(o[g], gp_ref[...]))
        h.append(_rms_scale(x[g], g1_ref[...]).astype(BF16))
        gu.append(gate_up(h[g], chunks[0]))
    for c, cols in enumerate(chunks):
        nxt = []
        for g in range(MIX_GROUPS):
            gate, up = gu[g]
            act = (gate * jax.nn.sigmoid(gate) * up).astype(BF16)
            if c + 1 < len(chunks):
                nxt.append(gate_up(h[g], chunks[c + 1]))
            down = dot(act, wdn_ref[cols, :])
            if c == 0:
                acc.append(down)
            else:
                acc[g] = acc[g] + down
        gu = nxt
    for g, r in enumerate(groups):
        o_ref[0, r, :] = x[g] + _rms_scale(acc[g], g2_ref[...])


def _mixffn(x, ys_sb, yd, yf, gs, gd, gf, wo, gp, g1, wgu, wdn, g2, *, tm, f_target):
    B, S, D = x.shape
    const = lambda b, j: (0, 0)
    tok = lambda w: pl.BlockSpec((1, tm, w), lambda b, j: (b, j, 0))
    full = lambda a: pl.BlockSpec(a.shape, const)
    weights = (gs, gd, gf, wo, gp, g1, wgu, wdn, g2)
    return pl.pallas_call(
        functools.partial(_mixffn_kernel, f_target=f_target),
        grid=(B, S // tm),
        in_specs=[tok(D), tok(SSM_WIDTH), tok(ATT_WIDTH),
                  tok(ATT_WIDTH)] + [full(a) for a in weights],
        out_specs=tok(D),
        out_shape=jax.ShapeDtypeStruct((B, S, D), F32),
        compiler_params=_params("parallel", "parallel"),
        name="mixffn",
    )(x, ys_sb, yd, yf, *weights)


def _s5_matrices(log_dt, a_re, a_im, b_re, b_im, c_re, c_im):
    G, P = a_re.shape
    C = b_re.shape[-1]
    dt = jnp.exp(log_dt)[:, None]
    decay = jnp.exp(a_re * dt)
    abar_re = decay * jnp.cos(a_im * dt)
    abar_im = decay * jnp.sin(a_im * dt)
    mag2 = a_re * a_re + a_im * a_im
    coef_re = ((abar_re - 1.0) * a_re + abar_im * a_im) / mag2
    coef_im = (abar_im * a_re - (abar_re - 1.0) * a_im) / mag2
    bbar_re = coef_re[..., None] * b_re - coef_im[..., None] * b_im
    bbar_im = coef_re[..., None] * b_im + coef_im[..., None] * b_re
    eye = jnp.eye(G, dtype=F32)
    bt_re = jnp.einsum('gpc,gh->gchp', bbar_re, eye).reshape(G * C, G * P)
    bt_im = jnp.einsum('gpc,gh->gchp', bbar_im, eye).reshape(G * C, G * P)
    bt = jnp.concatenate([bt_re, bt_im], axis=1).astype(BF16)
    ct_re = jnp.einsum('gcp,gh->gphc', c_re, eye).reshape(G * P, G * C)
    ct_im = jnp.einsum('gcp,gh->gphc', -c_im, eye).reshape(G * P, G * C)
    ct = jnp.concatenate([ct_re, ct_im], axis=0).astype(BF16)
    return bt, abar_re.reshape(1, G * P), abar_im.reshape(1, G * P), ct


def _per_pair_rows(v):
    out = jnp.zeros((HEAD_PAIRS, 8) + v.shape[1:], v.dtype)
    out = out.at[:, :2].set(v.reshape((HEAD_PAIRS, 2) + v.shape[1:]))
    return out.reshape((HEAD_PAIRS * 8,) + v.shape[1:])


def kernel(x, g_pre_mix, w_in, b_f, ssm_log_dt, ssm_a_re, ssm_a_im, ssm_b_re, ssm_b_im, ssm_c_re, ssm_c_im, ssm_d, w_glu, b_glu, g_mix, w_out, g_post_mix, g_pre_ffn, w_gate_up, w_down, g_post_ffn):
    B, S, D = x.shape
    depth = w_in.shape[0]
    o1 = SSM_WIDTH
    o2 = o1 + 3 * ATT_WIDTH
    o3 = o2 + 2 * ATT_WIDTH
    o4 = o3 + ATT_WIDTH
    row = lambda v: v.reshape(1, -1)

    for l in range(depth):
        w = w_in[l]
        u_ssm, qkv_dsa, k_fox, qt_fox, vt_fox, ft = _premix(
            x, row(g_pre_mix[l]), w[:, :o1].astype(BF16),
            jnp.concatenate([w[:, o1:o2], w[:, o2 + ATT_WIDTH:o3]], axis=1).astype(BF16),
            w[:, o2:o2 + ATT_WIDTH].T.astype(BF16), w[:, o3:o4].T.astype(BF16),
            _per_pair_rows(w[:, o4:].T).astype(BF16), tm=PREMIX_TILE)

        bt, a_re, a_im, ct = _s5_matrices(ssm_log_dt[l], ssm_a_re[l], ssm_a_im[l], ssm_b_re[l],
                                          ssm_b_im[l], ssm_c_re[l], ssm_c_im[l])
        y_ssm = _s5(u_ssm, bt, a_re, a_im, ct, row(ssm_d[l]), w_glu[l].astype(BF16),
                    row(b_glu[l]), steps=S5_STEPS)

        y_dsa = _dsa(qkv_dsa)

        bf = jnp.broadcast_to(_per_pair_rows(b_f[l][:, None]), (8 * HEAD_PAIRS, S))
        y_fox = _fox(k_fox, qt_fox, vt_fox, ft, bf, tq=FOX_BLOCK)

        gm = g_mix[l]
        x = _mixffn(x, y_ssm, y_dsa, y_fox,
                    row(gm[:o1]), row(gm[o1:o1 + ATT_WIDTH]), row(gm[o1 + ATT_WIDTH:]),
                    w_out[l].astype(BF16), row(g_post_mix[l]), row(g_pre_ffn[l]),
                    w_gate_up[l].astype(BF16), w_down[l].astype(BF16), row(g_post_ffn[l]),
                    tm=MIX_TILE, f_target=FFN_CHUNK)
    return x
```

```python
import functools

import jax
import jax.numpy as jnp
from jax import lax
from jax.experimental import pallas as pl
from jax.experimental.pallas import tpu as pltpu

F32 = jnp.float32
BF16 = jnp.bfloat16

HEAD_DIM = 64
SSM_WIDTH = 256
ATT_WIDTH = 384
N_HEADS = 6
HEAD_PAIRS = N_HEADS // 2
DSA_SPAN = 128
DSA_BLOCK = 128
DSA_DILATIONS = (1, 4, 16)
RMS_EPS = 1e-6
NEG = -1e30
LOG2E = 1.4426950408889634
Q_SCALE = HEAD_DIM ** -0.5 * LOG2E
PREMIX_TILE = 1024
MIX_TILE = 512
FFN_CHUNK = 1536
S5_STEPS = 64
FOX_BLOCK = 256
S5_READ_GROUPS = 4
PRE_GROUPS = 2
DSA_LOOKAHEAD = 3
FOX_LOOKAHEAD = 6

LANES = 128
BF16_SUBLANES = 16
VMEM_LIMIT = 56 * 1024 * 1024

NT_DIMS = (((1,), (1,)), ((), ()))


def _rms_scale(x, g):
    ms = jnp.mean(x * x, axis=-1, keepdims=True)
    return x * lax.rsqrt(ms + RMS_EPS) * g


def _params(*sem):
    return pltpu.CompilerParams(dimension_semantics=sem, vmem_limit_bytes=VMEM_LIMIT)


def _premix_kernel(x_ref, g_ref, wu_ref, wdk_ref, wqt_ref, wvt_ref, wft_ref,
                   u_ref, qd_ref, k_ref, qt_ref, vt_ref, ft_ref):
    tm = x_ref.shape[1]
    groups = [slice(i * tm // PRE_GROUPS, (i + 1) * tm // PRE_GROUPS) for i in range(PRE_GROUPS)]
    hs = [_rms_scale(x_ref[0, r, :], g_ref[...]).astype(BF16) for r in groups]
    for r, h in zip(groups, hs):
        u_ref[0, r, :] = jnp.dot(h, wu_ref[...], preferred_element_type=F32)
        dk = jnp.dot(h, wdk_ref[...], preferred_element_type=F32)
        qd_ref[0, r, :] = dk[:, :3 * ATT_WIDTH]
        k_ref[0, r, :] = dk[:, 3 * ATT_WIDTH:].astype(BF16)
        nt = lambda w_ref: lax.dot_general(w_ref[...], h, NT_DIMS, preferred_element_type=F32)
        qt_ref[0, :, r] = (nt(wqt_ref) * Q_SCALE).astype(BF16)
        vt_ref[0, :, r] = nt(wvt_ref).astype(BF16)
        ft_ref[0, :, r] = nt(wft_ref)


def _premix(x, g, wu, wdk, wqt, wvt, wft, *, tm):
    B, S, D = x.shape
    grid = (B, S // tm)
    const = lambda b, j: (0, 0)
    full = lambda a: pl.BlockSpec(a.shape, const)
    tok = lambda w: pl.BlockSpec((1, tm, w), lambda b, j: (b, j, 0))
    feat = lambda r: pl.BlockSpec((1, r, tm), lambda b, j: (b, 0, j))
    return pl.pallas_call(
        _premix_kernel,
        grid=grid,
        in_specs=[tok(D), full(g), full(wu), full(wdk), full(wqt), full(wvt), full(wft)],
        out_specs=[
            tok(SSM_WIDTH), tok(3 * ATT_WIDTH), tok(ATT_WIDTH), feat(ATT_WIDTH), feat(ATT_WIDTH),
            feat(8 * HEAD_PAIRS),
        ],
        out_shape=[
            jax.ShapeDtypeStruct((B, S, SSM_WIDTH), F32),
            jax.ShapeDtypeStruct((B, S, 3 * ATT_WIDTH), F32),
            jax.ShapeDtypeStruct((B, S, ATT_WIDTH), BF16),
            jax.ShapeDtypeStruct((B, ATT_WIDTH, S), BF16),
            jax.ShapeDtypeStruct((B, ATT_WIDTH, S), BF16),
            jax.ShapeDtypeStruct((B, 8 * HEAD_PAIRS, S), F32),
        ],
        compiler_params=_params("parallel", "parallel"),
        name="premix",
    )(x, g, wu, wdk, wqt, wvt, wft)


def _time_pitch(batch):
    assert batch % 8 == 0, batch
    return batch + 8 if (batch // 8) % 2 == 0 else batch


def _s5_kernel(u_ref, bt_ref, are_ref, aim_ref, ct_ref, d_ref, wg_ref, bg_ref, y_ref,
               tm_ref, xs_ref, h_ref, *, col_tile):
    batch, steps, width = u_ref.shape
    n_state = are_ref.shape[1]
    pitch = _time_pitch(batch)
    lane_tiles = [slice(t * LANES, (t + 1) * LANES) for t in range(width // LANES)]

    @pl.when(pl.program_id(0) == 0)
    def _():
        h_ref[...] = jnp.zeros_like(h_ref)

    for b in range(batch):
        for t, lanes in enumerate(lane_tiles):
            tm_ref[t, pl.ds(b, steps, stride=pitch), :] = u_ref[b, :, lanes]
    u = jnp.concatenate(
        [jnp.concatenate([tm_ref[t, s * pitch:s * pitch + batch, :] for s in range(steps)], axis=0)
         for t in range(len(lane_tiles))], axis=1)
    group_rows = steps // S5_READ_GROUPS * batch
    for gi in range(S5_READ_GROUPS):
        rows = slice(gi * group_rows, (gi + 1) * group_rows)
        xs_ref[rows, :] = jnp.dot(u[rows, :].astype(BF16), bt_ref[...], preferred_element_type=F32)

    for c in range(n_state // col_tile):
        re = slice(c * col_tile, (c + 1) * col_tile)
        im = slice(n_state + c * col_tile, n_state + (c + 1) * col_tile)
        ar = are_ref[:, re]
        ai = aim_ref[:, re]

        def step(s, carry, re=re, im=im, ar=ar, ai=ai):
            hr, hi = carry
            rows = pl.ds(pl.multiple_of(s * batch, batch), batch)
            nr = ar * hr - ai * hi + xs_ref[rows, re]
            ni = ar * hi + ai * hr + xs_ref[rows, im]
            xs_ref[rows, re] = nr
            xs_ref[rows, im] = ni
            return nr, ni

        hr, hi = lax.fori_loop(0, steps, step, (h_ref[:, re], h_ref[:, im]), unroll=2)
        h_ref[:, re] = hr
        h_ref[:, im] = hi

    group_steps = steps // S5_READ_GROUPS

    def project(gi):
        rows = slice(gi * group_steps * batch, (gi + 1) * group_steps * batch)
        y = jnp.dot(xs_ref[rows, :].astype(BF16), ct_ref[...], preferred_element_type=F32)
        return y + d_ref[...] * u[rows, :]

    def finish(gi, y):
        g = jax.nn.gelu(y)
        z = jnp.dot(g.astype(BF16), wg_ref[...], preferred_element_type=F32) + bg_ref[...]
        out = g * jax.nn.sigmoid(z)
        for t, lanes in enumerate(lane_tiles):
            for k in range(group_steps):
                s = gi * group_steps + k
                tm_ref[t, s * pitch:s * pitch + batch, :] = out[k * batch:(k + 1) * batch, lanes]

    y_next = project(0)
    for gi in range(S5_READ_GROUPS):
        y_cur = y_next
        if gi + 1 < S5_READ_GROUPS:
            y_next = project(gi + 1)
        finish(gi, y_cur)

    for b in range(batch):
        for t, lanes in enumerate(lane_tiles):
            y_ref[b, :, lanes] = tm_ref[t, pl.ds(b, steps, stride=pitch), :]


def _s5(u, bt, a_re, a_im, ct, d, wg, bg, *, steps):
    B, S, width = u.shape
    n_cols = bt.shape[1]
    const = lambda i: (0, 0)
    full = lambda a: pl.BlockSpec(a.shape, const)
    chunk = pl.BlockSpec((B, steps, width), lambda i: (0, i, 0))
    return pl.pallas_call(
        functools.partial(_s5_kernel, col_tile=2 * LANES),
        grid=(S // steps,),
        in_specs=[chunk, full(bt), full(a_re), full(a_im), full(ct), full(d), full(wg), full(bg)],
        out_specs=chunk,
        out_shape=jax.ShapeDtypeStruct((B, S, width), F32),
        scratch_shapes=[
            pltpu.VMEM((width // LANES, steps * _time_pitch(B), LANES), F32),
            pltpu.VMEM((steps * B, n_cols), F32),
            pltpu.VMEM((B, n_cols), F32),
        ],
        compiler_params=_params("arbitrary"),
        name="s5",
    )(u, bt, a_re, a_im, ct, d, wg, bg)


def _dsa_scores(qb, kw, valid, head0):
    out = []
    for h in range(2):
        qh = jnp.where(head0 if h == 0 else jnp.logical_not(head0), qb, jnp.zeros_like(qb))
        s = lax.dot_general(qh, kw, NT_DIMS, preferred_element_type=F32)
        out.append(jnp.where(valid, s, NEG))
    return out


def _dsa_finish(scores, vw, head0):
    vx = jnp.concatenate([vw, jnp.ones_like(vw)], axis=1)
    parts = []
    for s in scores:
        m = jnp.max(s, axis=-1, keepdims=True)
        r = jnp.dot(jnp.exp2(s - m).astype(BF16), vx, preferred_element_type=F32)
        parts.append((r[:, :LANES], jnp.broadcast_to(m, (s.shape[0], LANES)), r[:, LANES:]))
    return tuple(jnp.where(head0, a, b) for a, b in zip(*parts))


def _softmax_merge(n1, m1, l1, n2, m2, l2):
    m = jnp.maximum(m1, m2)
    e1 = jnp.exp2(m1 - m)
    e2 = jnp.exp2(m2 - m)
    return n1 * e1 + n2 * e2, m, l1 * e1 + l2 * e2


def _dsa_kernel(q_ref, k_ref, v_ref, o_ref, acc_ref, m_ref, l_ref,
                q4_ref, k4_ref, v4_ref, acc4_ref, m4_ref, l4_ref, *, seq):
    blk = DSA_BLOCK
    n_blocks = seq // blk
    head0 = lax.broadcasted_iota(jnp.int32, (blk, LANES), 1) < HEAD_DIM
    qi = lax.broadcasted_iota(jnp.int32, (blk, 2 * blk), 0)
    ki = lax.broadcasted_iota(jnp.int32, (blk, 2 * blk), 1)
    band = (ki >= qi) & (ki <= qi + DSA_SPAN)
    causal = (lax.broadcasted_iota(jnp.int32, (blk, blk), 1)
              <= lax.broadcasted_iota(jnp.int32, (blk, blk), 0))

    def bf(x, scale=None):
        return (x if scale is None else x * scale).astype(BF16)

    pending = []

    def retire():
        scores, vw, sink = pending.pop(0)
        sink(_dsa_finish(scores, vw, head0))

    def submit(q, kw, vw, sink):
        valid = band if kw.shape[0] == 2 * blk else causal
        pending.append((_dsa_scores(bf(q, Q_SCALE), bf(kw), valid, head0), bf(vw), sink))
        if len(pending) > DSA_LOOKAHEAD:
            retire()

    def drain():
        while pending:
            retire()

    def window(ref_rows, start, has_prev):
        return ref_rows(start - blk, 2 * blk) if has_prev else ref_rows(start, blk)

    nat = lambda ref: (lambda start, n: ref[0, start:start + n, :])
    seg = lambda ref: (lambda start, n: ref[start:start + n, :])
    state = (acc_ref, m_ref, l_ref)
    state4 = (acc4_ref, m4_ref, l4_ref)

    d_mid = DSA_DILATIONS[1]
    seg_len = seq // d_mid
    for rho in range(d_mid):
        for src, dst_ref in ((q_ref, q4_ref), (k_ref, k4_ref), (v_ref, v4_ref)):
            dst_ref[rho * seg_len:(rho + 1) * seg_len, :] = src[0, pl.ds(rho, seg_len, stride=d_mid), :]

    def init_state(rows):
        def sink(new):
            for ref, val in zip(state, new):
                ref[rows, :] = val
        return sink

    for n in range(n_blocks):
        start = n * blk
        submit(nat(q_ref)(start, blk), window(nat(k_ref), start, n > 0),
               window(nat(v_ref), start, n > 0), init_state(slice(start, start + blk)))
    drain()

    for rho in range(d_mid):
        for src, dst_ref in zip(state, state4):
            dst_ref[rho * seg_len:(rho + 1) * seg_len, :] = src[pl.ds(rho, seg_len, stride=d_mid), :]

    def merge_state(rows):
        def sink(new):
            old = tuple(ref[rows, :] for ref in state4)
            for ref, val in zip(state4, _softmax_merge(*old, *new)):
                ref[rows, :] = val
        return sink

    for rho in range(d_mid):
        for n in range(seg_len // blk):
            start = rho * seg_len + n * blk
            submit(seg(q4_ref)(start, blk), window(seg(k4_ref), start, n > 0),
                   window(seg(v4_ref), start, n > 0), merge_state(slice(start, start + blk)))
    drain()

    def finalize(rows):
        def sink(new):
            old = tuple(ref[rows, :] for ref in state4)
            num, _, den = _softmax_merge(*old, *new)
            acc4_ref[rows, :] = num / den
        return sink

    inner = DSA_DILATIONS[2] // d_mid
    for rho in range(d_mid):
        for c in range(inner):
            rs = pl.ds(rho * seg_len + c, blk, stride=inner)
            submit(q4_ref[rs, :], k4_ref[rs, :], v4_ref[rs, :], finalize(rs))
    drain()

    for rho in range(d_mid):
        o_ref[0, pl.ds(rho, seg_len, stride=d_mid), :] = acc4_ref[rho * seg_len:(rho + 1) * seg_len, :]


def _dsa(qkv):
    B, S, _ = qkv.shape
    assert S == DSA_DILATIONS[2] * DSA_BLOCK, S
    blockspec = lambda off: pl.BlockSpec((1, S, LANES), lambda b, p: (b, 0, off + p))
    return pl.pallas_call(
        functools.partial(_dsa_kernel, seq=S),
        grid=(B, HEAD_PAIRS),
        in_specs=[blockspec(0), blockspec(HEAD_PAIRS), blockspec(2 * HEAD_PAIRS)],
        out_specs=blockspec(0),
        out_shape=jax.ShapeDtypeStruct((B, S, ATT_WIDTH), F32),
        scratch_shapes=[pltpu.VMEM((S, LANES), F32)] * 9,
        compiler_params=_params("parallel", "parallel"),
        name="dsa",
    )(qkv, qkv, qkv)


def _bf16_pieces(x):
    hi = x.astype(BF16).astype(F32)
    r1 = x - hi
    lo = r1.astype(BF16).astype(F32)
    return hi, lo, (r1 - lo).astype(BF16).astype(F32)


def _lane_cumsum(x, block):
    n = x.shape[1]
    starts = range(0, n, block)
    pieces = _bf16_pieces(x)
    lhs = jnp.concatenate([p[:, a:a + block] for a in starts for p in pieces], axis=0)
    tri = (lax.broadcasted_iota(jnp.int32, (block, block), 0)
           <= lax.broadcasted_iota(jnp.int32, (block, block), 1)).astype(BF16)
    res = jnp.dot(lhs.astype(BF16), tri, preferred_element_type=F32)
    rows = x.shape[0]
    within = [res[3 * rows * j:3 * rows * j + rows] + res[3 * rows * j + rows:3 * rows * j + 2 * rows]
              + res[3 * rows * j + 2 * rows:3 * rows * (j + 1)] for j in range(len(starts))]
    out, offset = [], None
    for blk in within:
        out.append(blk if offset is None else blk + offset)
        total = blk[:, block - 1:block]
        offset = total if offset is None else offset + total
    return jnp.concatenate(out, axis=1)


def _fox_kernel(k_ref, qt_ref, vt_ref, ft_ref, bf_ref, o_ref, qa_ref, ka_ref, vx_ref, *, seq, tq):
    x = ft_ref[0] + bf_ref[...]
    logf = jnp.minimum(x, 0.0) - jnp.log1p(jnp.exp(-jnp.abs(x)))
    cum = _lane_cumsum(logf, 2 * LANES)

    hi, lo, lo2 = _bf16_pieces(cum * LOG2E)

    sub = lax.broadcasted_iota(jnp.int32, (8, seq), 0)
    rows8 = lambda v, r: jnp.broadcast_to(v[r:r + 1, :], (8, seq))
    pad = jnp.zeros((HEAD_DIM - 8, seq), F32)
    q_tiles, k_tiles = [], []
    for r in range(2):
        h8, l8, l28 = rows8(hi, r), rows8(lo, r), rows8(lo2, r)
        q_tiles += [jnp.where(sub == 0, h8, jnp.where(sub == 1, l8, jnp.where(sub == 2, l28,
                              jnp.where(sub < 6, 1.0, 0.0)))), pad]
        k_tiles += [jnp.where(sub < 3, 1.0, jnp.where(sub == 3, -h8, jnp.where(sub == 4, -l8,
                              jnp.where(sub == 5, -l28, 0.0)))), pad]
    aq_t = jnp.concatenate(q_tiles, axis=0).astype(BF16)
    ak = jnp.concatenate(k_tiles, axis=0).T

    lane = lax.broadcasted_iota(jnp.int32, (1, LANES), 1)
    qt = qt_ref[0]
    kv = k_ref[0]
    zeros_t = jnp.zeros((HEAD_DIM, seq), BF16)
    for h in range(2):
        in_head = (lane >= HEAD_DIM * h) & (lane < HEAD_DIM * (h + 1))
        qa_ref[h, :LANES, :] = (jnp.concatenate([qt[:HEAD_DIM], zeros_t], axis=0) if h == 0 else
                                jnp.concatenate([zeros_t, qt[HEAD_DIM:]], axis=0))
        qa_ref[h, LANES:, :] = aq_t
        ka_ref[h, :, :LANES] = kv
        ka_ref[h, :, LANES:] = (ak * in_head.astype(F32)).astype(BF16)

    for h in range(2):
        vx_ref[h, :HEAD_DIM, :] = vt_ref[0, HEAD_DIM * h:HEAD_DIM * (h + 1), :]
        vx_ref[h, HEAD_DIM:, :] = jnp.ones((BF16_SUBLANES, seq), BF16)

    key_le_query = (lax.broadcasted_iota(jnp.int32, (tq, tq), 0)
                    <= lax.broadcasted_iota(jnp.int32, (tq, tq), 1))

    n_q = seq // tq
    assert n_q % 2 == 0, n_q
    steps = []
    for t in range(n_q + 1):
        for a in range(n_q // 2):
            j, iq = (t, a) if t <= a else (t - a - 1, n_q - 1 - a)
            steps += [(j, iq, 0), (j, iq, 1)]
    state = {}
    outs = {}

    def scores(j, iq, h):
        s = jnp.dot(ka_ref[h, j * tq:(j + 1) * tq, :], qa_ref[h, :, iq * tq:(iq + 1) * tq],
                    preferred_element_type=F32)
        return jnp.where(key_le_query, s, NEG) if j == iq else s

    def absorb(j, iq, h, s):
        m = jnp.max(s, axis=0, keepdims=True)
        if j > 0:
            m_old, acc_old = state[iq, h]
            m = jnp.maximum(m_old, m)
        acc = jnp.dot(vx_ref[h, :, j * tq:(j + 1) * tq], jnp.exp2(s - m).astype(BF16),
                      preferred_element_type=F32)
        if j > 0:
            acc = acc_old * jnp.exp2(m_old - m) + acc
        state[iq, h] = (m, acc)
        if j == iq:
            del state[iq, h]
            outs[iq, h] = acc[:HEAD_DIM, :] / acc[HEAD_DIM:HEAD_DIM + 1, :]
            if h == 1:
                o_ref[0, iq * tq:(iq + 1) * tq, :] = jnp.concatenate(
                    [outs.pop((iq, 0)), outs.pop((iq, 1))], axis=0).T

    pending = []
    for step in steps:
        pending.append((step, scores(*step)))
        if len(pending) > FOX_LOOKAHEAD:
            done, s = pending.pop(0)
            absorb(*done, s)
    for done, s in pending:
        absorb(*done, s)


def _fox(k, qt, vt, ft, bf, *, tq):
    B, S, _ = k.shape
    tok = pl.BlockSpec((1, S, LANES), lambda b, p: (b, 0, p))
    feat = lambda rows: pl.BlockSpec((1, rows, S), lambda b, p: (b, p, 0))
    return pl.pallas_call(
        functools.partial(_fox_kernel, seq=S, tq=tq),
        grid=(B, HEAD_PAIRS),
        in_specs=[tok, feat(LANES), feat(LANES), feat(8),
                  pl.BlockSpec((8, S), lambda b, p: (p, 0))],
        out_specs=tok,
        out_shape=jax.ShapeDtypeStruct((B, S, ATT_WIDTH), F32),
        scratch_shapes=[pltpu.VMEM((2, 2 * LANES, S), BF16),
                        pltpu.VMEM((2, S, 2 * LANES), BF16),
                        pltpu.VMEM((2, HEAD_DIM + BF16_SUBLANES, S), BF16)],
        compiler_params=_params("parallel", "parallel"),
        name="fox",
    )(k, qt, vt, ft, bf)


MXU_WIDTH = 256
MIX_GROUPS = 2


def _hidden_chunks(d_ff, target):
    assert d_ff % MXU_WIDTH == 0, d_ff
    tiles = d_ff // MXU_WIDTH
    n = -(-d_ff // target)
    sizes = [(tiles // n + (i < tiles % n)) * MXU_WIDTH for i in range(n)]
    starts = [sum(sizes[:i]) for i in range(n)]
    return [slice(a, a + w) for a, w in zip(starts, sizes)]


def _mixffn_kernel(x_ref, ys_ref, yd_ref, yf_ref, gs_ref, gd_ref, gf_ref, wo_ref,
                   gp_ref, g1_ref, wgu_ref, wdn_ref, g2_ref, o_ref, *, f_target):
    tm = x_ref.shape[1]
    groups = [slice(i * tm // MIX_GROUPS, (i + 1) * tm // MIX_GROUPS) for i in range(MIX_GROUPS)]
    dot = lambda a, b: jnp.dot(a, b, preferred_element_type=F32)

    def mix_in(r):
        mixed = jnp.concatenate([_rms_scale(ys_ref[0, r, :], gs_ref[...]).astype(BF16),
                                 _rms_scale(yd_ref[0, r, :], gd_ref[...]).astype(BF16),
                                 _rms_scale(yf_ref[0, r, :], gf_ref[...]).astype(BF16)], axis=1)
        return dot(mixed, wo_ref[...])

    o = [mix_in(r) for r in groups]
    x, h, acc = [], [], []
    d_ff = wdn_ref.shape[0]
    chunks = _hidden_chunks(d_ff, f_target)
    gate_up = lambda hg, cols: (dot(hg, wgu_ref[:, cols]),
                                dot(hg, wgu_ref[:, d_ff + cols.start:d_ff + cols.stop]))
    gu = []
    for g, r in enumerate(groups):
        x.append(x_ref[0, r, :] + _rms_scale(o[g], gp_ref[...]))
        h.append(_rms_scale(x[g], g1_ref[...]).astype(BF16))
        gu.append(gate_up(h[g], chunks[0]))
    for c, cols in enumerate(chunks):
        nxt = []
        for g in range(MIX_GROUPS):
            gate, up = gu[g]
            act = (gate * jax.nn.sigmoid(gate) * up).astype(BF16)
            if c + 1 < len(chunks):
                nxt.append(gate_up(h[g], chunks[c + 1]))
            down = dot(act, wdn_ref[cols, :])
            if c == 0:
                acc.append(down)
            else:
                acc[g] = acc[g] + down
        gu = nxt
    for g, r in enumerate(groups):
        o_ref[0, r, :] = x[g] + _rms_scale(acc[g], g2_ref[...])


def _mixffn(x, ys_sb, yd, yf, gs, gd, gf, wo, gp, g1, wgu, wdn, g2, *, layer, tm, f_target):
    B, S, D = x.shape
    const = lambda b, j: (0, 0)
    tok = lambda w: pl.BlockSpec((1, tm, w), lambda b, j: (b, j, 0))

    def full(a):
        if a.ndim == 2:
            return pl.BlockSpec(a.shape, const)
        return pl.BlockSpec((None,) + a.shape[1:], lambda b, j: (layer, 0, 0),
                            pipeline_mode=pl.Buffered(1))

    weights = (gs, gd, gf, wo, gp, g1, wgu, wdn, g2)
    return pl.pallas_call(
        functools.partial(_mixffn_kernel, f_target=f_target),
        grid=(B, S // tm),
        in_specs=[tok(D), tok(SSM_WIDTH), tok(ATT_WIDTH),
                  tok(ATT_WIDTH)] + [full(a) for a in weights],
        out_specs=tok(D),
        out_shape=jax.ShapeDtypeStruct((B, S, D), F32),
        compiler_params=_params("parallel", "parallel"),
        name="mixffn",
    )(x, ys_sb, yd, yf, *weights)


def _s5_matrices(log_dt, a_re, a_im, b_re, b_im, c_re, c_im):
    G, P = a_re.shape
    C = b_re.shape[-1]
    dt = jnp.exp(log_dt)[:, None]
    decay = jnp.exp(a_re * dt)
    abar_re = decay * jnp.cos(a_im * dt)
    abar_im = decay * jnp.sin(a_im * dt)
    mag2 = a_re * a_re + a_im * a_im
    coef_re = ((abar_re - 1.0) * a_re + abar_im * a_im) / mag2
    coef_im = (abar_im * a_re - (abar_re - 1.0) * a_im) / mag2
    bbar_re = coef_re[..., None] * b_re - coef_im[..., None] * b_im
    bbar_im = coef_re[..., None] * b_im + coef_im[..., None] * b_re
    eye = jnp.eye(G, dtype=F32)
    bt_re = jnp.einsum('gpc,gh->gchp', bbar_re, eye).reshape(G * C, G * P)
    bt_im = jnp.einsum('gpc,gh->gchp', bbar_im, eye).reshape(G * C, G * P)
    bt = jnp.concatenate([bt_re, bt_im], axis=1).astype(BF16)
    ct_re = jnp.einsum('gcp,gh->gphc', c_re, eye).reshape(G * P, G * C)
    ct_im = jnp.einsum('gcp,gh->gphc', -c_im, eye).reshape(G * P, G * C)
    ct = jnp.concatenate([ct_re, ct_im], axis=0).astype(BF16)
    return bt, abar_re.reshape(1, G * P), abar_im.reshape(1, G * P), ct


def _per_pair_rows(v):
    out = jnp.zeros((HEAD_PAIRS, 8) + v.shape[1:], v.dtype)
    out = out.at[:, :2].set(v.reshape((HEAD_PAIRS, 2) + v.shape[1:]))
    return out.reshape((HEAD_PAIRS * 8,) + v.shape[1:])


def kernel(x, g_pre_mix, w_in, b_f, ssm_log_dt, ssm_a_re, ssm_a_im, ssm_b_re, ssm_b_im, ssm_c_re, ssm_c_im, ssm_d, w_glu, b_glu, g_mix, w_out, g_post_mix, g_pre_ffn, w_gate_up, w_down, g_post_ffn):
    B, S, D = x.shape
    depth = w_in.shape[0]
    o1 = SSM_WIDTH
    o2 = o1 + 3 * ATT_WIDTH
    o3 = o2 + 2 * ATT_WIDTH
    o4 = o3 + ATT_WIDTH
    row = lambda v: v.reshape(1, -1)
    w_out_bf, w_gate_up_bf, w_down_bf = (w.astype(BF16) for w in (w_out, w_gate_up, w_down))

    for l in range(depth):
        w = w_in[l]
        u_ssm, qkv_dsa, k_fox, qt_fox, vt_fox, ft = _premix(
            x, row(g_pre_mix[l]), w[:, :o1].astype(BF16),
            jnp.concatenate([w[:, o1:o2], w[:, o2 + ATT_WIDTH:o3]], axis=1).astype(BF16),
            w[:, o2:o2 + ATT_WIDTH].T.astype(BF16), w[:, o3:o4].T.astype(BF16),
            _per_pair_rows(w[:, o4:].T).astype(BF16), tm=PREMIX_TILE)

        bt, a_re, a_im, ct = _s5_matrices(ssm_log_dt[l], ssm_a_re[l], ssm_a_im[l], ssm_b_re[l],
                                          ssm_b_im[l], ssm_c_re[l], ssm_c_im[l])
        y_ssm = _s5(u_ssm, bt, a_re, a_im, ct, row(ssm_d[l]), w_glu[l].astype(BF16),
                    row(b_glu[l]), steps=S5_STEPS)

        y_dsa = _dsa(qkv_dsa)

        bf = jnp.broadcast_to(_per_pair_rows(b_f[l][:, None]), (8 * HEAD_PAIRS, S))
        y_fox = _fox(k_fox, qt_fox, vt_fox, ft, bf, tq=FOX_BLOCK)

        gm = g_mix[l]
        x = _mixffn(x, y_ssm, y_dsa, y_fox,
                    row(gm[:o1]), row(gm[o1:o1 + ATT_WIDTH]), row(gm[o1 + ATT_WIDTH:]),
                    w_out_bf, row(g_post_mix[l]), row(g_pre_ffn[l]), w_gate_up_bf, w_down_bf,
                    row(g_post_ffn[l]), layer=l, tm=MIX_TILE, f_target=FFN_CHUNK)
    return x
```

```python
import functools

import jax
import jax.numpy as jnp
from jax import lax
from jax.experimental import pallas as pl
from jax.experimental.pallas import tpu as pltpu

F32 = jnp.float32
BF16 = jnp.bfloat16

HEAD_DIM = 64
SSM_WIDTH = 256
ATT_WIDTH = 384
N_HEADS = 6
HEAD_PAIRS = N_HEADS // 2
DSA_SPAN = 128
DSA_BLOCK = 128
DSA_DILATIONS = (1, 4, 16)
RMS_EPS = 1e-6
NEG = -1e30
LOG2E = 1.4426950408889634
Q_SCALE = HEAD_DIM ** -0.5 * LOG2E
PREMIX_TILE = 1024
MIX_TILE = 512
FFN_CHUNK = 1536
S5_STEPS = 64
FOX_BLOCK = 256
S5_READ_GROUPS = 4
PRE_GROUPS = 2
DSA_LOOKAHEAD = 3
FOX_LOOKAHEAD = 6

LANES = 128
BF16_SUBLANES = 16
VMEM_LIMIT = 56 * 1024 * 1024

NT_DIMS = (((1,), (1,)), ((), ()))


def _rms_scale(x, g):
    ms = jnp.mean(x * x, axis=-1, keepdims=True)
    return x * lax.rsqrt(ms + RMS_EPS) * g


def _params(*sem):
    return pltpu.CompilerParams(dimension_semantics=sem, vmem_limit_bytes=VMEM_LIMIT)


def _premix_kernel(x_ref, g_ref, wu_ref, wdk_ref, wqt_ref, wvt_ref, wft_ref,
                   u_ref, qd_ref, k_ref, qt_ref, vt_ref, ft_ref):
    tm = x_ref.shape[1]
    groups = [slice(i * tm // PRE_GROUPS, (i + 1) * tm // PRE_GROUPS) for i in range(PRE_GROUPS)]
    hs = [_rms_scale(x_ref[0, r, :], g_ref[...]).astype(BF16) for r in groups]
    for r, h in zip(groups, hs):
        u_ref[0, r, :] = jnp.dot(h, wu_ref[...], preferred_element_type=F32)
        dk = jnp.dot(h, wdk_ref[...], preferred_element_type=F32)
        qd_ref[0, r, :] = dk[:, :3 * ATT_WIDTH]
        k_ref[0, r, :] = dk[:, 3 * ATT_WIDTH:].astype(BF16)
        nt = lambda w_ref: lax.dot_general(w_ref[...], h, NT_DIMS, preferred_element_type=F32)
        qt_ref[0, :, r] = (nt(wqt_ref) * Q_SCALE).astype(BF16)
        vt_ref[0, :, r] = nt(wvt_ref).astype(BF16)
        ft_ref[0, :, r] = nt(wft_ref)


def _premix(x, g, wu, wdk, wqt, wvt, wft, *, tm):
    B, S, D = x.shape
    grid = (B, S // tm)
    const = lambda b, j: (0, 0)
    full = lambda a: pl.BlockSpec(a.shape, const)
    tok = lambda w: pl.BlockSpec((1, tm, w), lambda b, j: (b, j, 0))
    feat = lambda r: pl.BlockSpec((1, r, tm), lambda b, j: (b, 0, j))
    return pl.pallas_call(
        _premix_kernel,
        grid=grid,
        in_specs=[tok(D), full(g), full(wu), full(wdk), full(wqt), full(wvt), full(wft)],
        out_specs=[
            tok(SSM_WIDTH), tok(3 * ATT_WIDTH), tok(ATT_WIDTH), feat(ATT_WIDTH), feat(ATT_WIDTH),
            feat(8 * HEAD_PAIRS),
        ],
        out_shape=[
            jax.ShapeDtypeStruct((B, S, SSM_WIDTH), F32),
            jax.ShapeDtypeStruct((B, S, 3 * ATT_WIDTH), F32),
            jax.ShapeDtypeStruct((B, S, ATT_WIDTH), BF16),
            jax.ShapeDtypeStruct((B, ATT_WIDTH, S), BF16),
            jax.ShapeDtypeStruct((B, ATT_WIDTH, S), BF16),
            jax.ShapeDtypeStruct((B, 8 * HEAD_PAIRS, S), F32),
        ],
        compiler_params=_params("parallel", "parallel"),
        name="premix",
    )(x, g, wu, wdk, wqt, wvt, wft)


def _time_pitch(batch):
    assert batch % 8 == 0, batch
    return batch + 8 if (batch // 8) % 2 == 0 else batch


def _s5_kernel(u_ref, bt_ref, are_ref, aim_ref, ct_ref, d_ref, wg_ref, bg_ref, y_ref,
               tm_ref, xs_ref, h_ref, *, col_tile):
    batch, steps, width = u_ref.shape
    n_state = are_ref.shape[1]
    pitch = _time_pitch(batch)
    lane_tiles = [slice(t * LANES, (t + 1) * LANES) for t in range(width // LANES)]

    @pl.when(pl.program_id(0) == 0)
    def _():
        h_ref[...] = jnp.zeros_like(h_ref)

    for b in range(batch):
        for t, lanes in enumerate(lane_tiles):
            tm_ref[t, pl.ds(b, steps, stride=pitch), :] = u_ref[b, :, lanes]
    u = jnp.concatenate(
        [jnp.concatenate([tm_ref[t, s * pitch:s * pitch + batch, :] for s in range(steps)], axis=0)
         for t in range(len(lane_tiles))], axis=1)
    group_rows = steps // S5_READ_GROUPS * batch
    for gi in range(S5_READ_GROUPS):
        rows = slice(gi * group_rows, (gi + 1) * group_rows)
        xs_ref[rows, :] = jnp.dot(u[rows, :].astype(BF16), bt_ref[...], preferred_element_type=F32)

    for c in range(n_state // col_tile):
        re = slice(c * col_tile, (c + 1) * col_tile)
        im = slice(n_state + c * col_tile, n_state + (c + 1) * col_tile)
        ar = are_ref[:, re]
        ai = aim_ref[:, re]

        def step(s, carry, re=re, im=im, ar=ar, ai=ai):
            hr, hi = carry
            rows = pl.ds(pl.multiple_of(s * batch, batch), batch)
            nr = ar * hr - ai * hi + xs_ref[rows, re]
            ni = ar * hi + ai * hr + xs_ref[rows, im]
            xs_ref[rows, re] = nr
            xs_ref[rows, im] = ni
            return nr, ni

        hr, hi = lax.fori_loop(0, steps, step, (h_ref[:, re], h_ref[:, im]), unroll=2)
        h_ref[:, re] = hr
        h_ref[:, im] = hi

    group_steps = steps // S5_READ_GROUPS

    def project(gi):
        rows = slice(gi * group_steps * batch, (gi + 1) * group_steps * batch)
        y = jnp.dot(xs_ref[rows, :].astype(BF16), ct_ref[...], preferred_element_type=F32)
        return y + d_ref[...] * u[rows, :]

    def finish(gi, y):
        g = jax.nn.gelu(y)
        z = jnp.dot(g.astype(BF16), wg_ref[...], preferred_element_type=F32) + bg_ref[...]
        out = g * jax.nn.sigmoid(z)
        for t, lanes in enumerate(lane_tiles):
            for k in range(group_steps):
                s = gi * group_steps + k
                tm_ref[t, s * pitch:s * pitch + batch, :] = out[k * batch:(k + 1) * batch, lanes]

    y_next = project(0)
    for gi in range(S5_READ_GROUPS):
        y_cur = y_next
        if gi + 1 < S5_READ_GROUPS:
            y_next = project(gi + 1)
        finish(gi, y_cur)

    for b in range(batch):
        for t, lanes in enumerate(lane_tiles):
            y_ref[b, :, lanes] = tm_ref[t, pl.ds(b, steps, stride=pitch), :]


def _s5(u, bt, a_re, a_im, ct, d, wg, bg, *, steps):
    B, S, width = u.shape
    n_cols = bt.shape[1]
    const = lambda i: (0, 0)
    full = lambda a: pl.BlockSpec(a.shape, const)
    chunk = pl.BlockSpec((B, steps, width), lambda i: (0, i, 0))
    return pl.pallas_call(
        functools.partial(_s5_kernel, col_tile=2 * LANES),
        grid=(S // steps,),
        in_specs=[chunk, full(bt), full(a_re), full(a_im), full(ct), full(d), full(wg), full(bg)],
        out_specs=chunk,
        out_shape=jax.ShapeDtypeStruct((B, S, width), F32),
        scratch_shapes=[
            pltpu.VMEM((width // LANES, steps * _time_pitch(B), LANES), F32),
            pltpu.VMEM((steps * B, n_cols), F32),
            pltpu.VMEM((B, n_cols), F32),
        ],
        compiler_params=_params("arbitrary"),
        name="s5",
    )(u, bt, a_re, a_im, ct, d, wg, bg)


def _dsa_scores(qb, kw, valid, head0):
    out = []
    for h in range(2):
        qh = jnp.where(head0 if h == 0 else jnp.logical_not(head0), qb, jnp.zeros_like(qb))
        s = lax.dot_general(qh, kw, NT_DIMS, preferred_element_type=F32)
        out.append(jnp.where(valid, s, NEG))
    return out


def _dsa_finish(scores, vw, head0):
    vx = jnp.concatenate([vw, jnp.ones_like(vw)], axis=1)
    parts = []
    for s in scores:
        m = jnp.max(s, axis=-1, keepdims=True)
        r = jnp.dot(jnp.exp2(s - m).astype(BF16), vx, preferred_element_type=F32)
        parts.append((r[:, :LANES], jnp.broadcast_to(m, (s.shape[0], LANES)), r[:, LANES:]))
    return tuple(jnp.where(head0, a, b) for a, b in zip(*parts))


def _softmax_merge(n1, m1, l1, n2, m2, l2):
    m = jnp.maximum(m1, m2)
    e1 = jnp.exp2(m1 - m)
    e2 = jnp.exp2(m2 - m)
    return n1 * e1 + n2 * e2, m, l1 * e1 + l2 * e2


def _dsa_kernel(q_ref, k_ref, v_ref, o_ref, acc_ref, m_ref, l_ref,
                q4_ref, k4_ref, v4_ref, acc4_ref, m4_ref, l4_ref, *, seq):
    blk = DSA_BLOCK
    n_blocks = seq // blk
    head0 = lax.broadcasted_iota(jnp.int32, (blk, LANES), 1) < HEAD_DIM
    qi = lax.broadcasted_iota(jnp.int32, (blk, 2 * blk), 0)
    ki = lax.broadcasted_iota(jnp.int32, (blk, 2 * blk), 1)
    band = (ki >= qi) & (ki <= qi + DSA_SPAN)
    causal = (lax.broadcasted_iota(jnp.int32, (blk, blk), 1)
              <= lax.broadcasted_iota(jnp.int32, (blk, blk), 0))

    def bf(x, scale=None):
        return (x if scale is None else x * scale).astype(BF16)

    pending = []

    def retire():
        scores, vw, sink = pending.pop(0)
        sink(_dsa_finish(scores, vw, head0))

    def submit(q, kw, vw, sink):
        valid = band if kw.shape[0] == 2 * blk else causal
        pending.append((_dsa_scores(bf(q, Q_SCALE), bf(kw), valid, head0), bf(vw), sink))
        if len(pending) > DSA_LOOKAHEAD:
            retire()

    def drain():
        while pending:
            retire()

    def window(ref_rows, start, has_prev):
        return ref_rows(start - blk, 2 * blk) if has_prev else ref_rows(start, blk)

    nat = lambda ref: (lambda start, n: ref[0, start:start + n, :])
    seg = lambda ref: (lambda start, n: ref[start:start + n, :])
    state = (acc_ref, m_ref, l_ref)
    state4 = (acc4_ref, m4_ref, l4_ref)

    d_mid = DSA_DILATIONS[1]
    seg_len = seq // d_mid
    for rho in range(d_mid):
        for src, dst_ref in ((q_ref, q4_ref), (k_ref, k4_ref), (v_ref, v4_ref)):
            dst_ref[rho * seg_len:(rho + 1) * seg_len, :] = src[0, pl.ds(rho, seg_len, stride=d_mid), :]

    def init_state(rows):
        def sink(new):
            for ref, val in zip(state, new):
                ref[rows, :] = val
        return sink

    for n in range(n_blocks):
        start = n * blk
        submit(nat(q_ref)(start, blk), window(nat(k_ref), start, n > 0),
               window(nat(v_ref), start, n > 0), init_state(slice(start, start + blk)))
    drain()

    for rho in range(d_mid):
        for src, dst_ref in zip(state, state4):
            dst_ref[rho * seg_len:(rho + 1) * seg_len, :] = src[pl.ds(rho, seg_len, stride=d_mid), :]

    def merge_state(rows):
        def sink(new):
            old = tuple(ref[rows, :] for ref in state4)
            for ref, val in zip(state4, _softmax_merge(*old, *new)):
                ref[rows, :] = val
        return sink

    for rho in range(d_mid):
        for n in range(seg_len // blk):
            start = rho * seg_len + n * blk
            submit(seg(q4_ref)(start, blk), window(seg(k4_ref), start, n > 0),
                   window(seg(v4_ref), start, n > 0), merge_state(slice(start, start + blk)))
    drain()

    def finalize(rows):
        def sink(new):
            old = tuple(ref[rows, :] for ref in state4)
            num, _, den = _softmax_merge(*old, *new)
            acc4_ref[rows, :] = num / den
        return sink

    inner = DSA_DILATIONS[2] // d_mid
    for rho in range(d_mid):
        for c in range(inner):
            rs = pl.ds(rho * seg_len + c, blk, stride=inner)
            submit(q4_ref[rs, :], k4_ref[rs, :], v4_ref[rs, :], finalize(rs))
    drain()

    for rho in range(d_mid):
        o_ref[0, pl.ds(rho, seg_len, stride=d_mid), :] = acc4_ref[rho * seg_len:(rho + 1) * seg_len, :]


def _dsa(qkv):
    B, S, _ = qkv.shape
    assert S == DSA_DILATIONS[2] * DSA_BLOCK, S
    blockspec = lambda off: pl.BlockSpec((1, S, LANES), lambda b, p: (b, 0, off + p))
    return pl.pallas_call(
        functools.partial(_dsa_kernel, seq=S),
        grid=(B, HEAD_PAIRS),
        in_specs=[blockspec(0), blockspec(HEAD_PAIRS), blockspec(2 * HEAD_PAIRS)],
        out_specs=blockspec(0),
        out_shape=jax.ShapeDtypeStruct((B, S, ATT_WIDTH), F32),
        scratch_shapes=[pltpu.VMEM((S, LANES), F32)] * 9,
        compiler_params=_params("parallel", "parallel"),
        name="dsa",
    )(qkv, qkv, qkv)


def _bf16_pieces(x):
    hi = x.astype(BF16).astype(F32)
    r1 = x - hi
    lo = r1.astype(BF16).astype(F32)
    return hi, lo, (r1 - lo).astype(BF16).astype(F32)


def _lane_cumsum(x, block):
    n = x.shape[1]
    starts = range(0, n, block)
    pieces = _bf16_pieces(x)
    lhs = jnp.concatenate([p[:, a:a + block] for a in starts for p in pieces], axis=0)
    tri = (lax.broadcasted_iota(jnp.int32, (block, block), 0)
           <= lax.broadcasted_iota(jnp.int32, (block, block), 1)).astype(BF16)
    res = jnp.dot(lhs.astype(BF16), tri, preferred_element_type=F32)
    rows = x.shape[0]
    within = [res[3 * rows * j:3 * rows * j + rows] + res[3 * rows * j + rows:3 * rows * j + 2 * rows]
              + res[3 * rows * j + 2 * rows:3 * rows * (j + 1)] for j in range(len(starts))]
    out, offset = [], None
    for blk in within:
        out.append(blk if offset is None else blk + offset)
        total = blk[:, block - 1:block]
        offset = total if offset is None else offset + total
    return jnp.concatenate(out, axis=1)


def _fox_kernel(k_ref, qt_ref, vt_ref, ft_ref, bf_ref, o_ref, qa_ref, ka_ref, vx_ref, *, seq, tq):
    x = ft_ref[0] + bf_ref[...]
    logf = jnp.minimum(x, 0.0) - jnp.log1p(jnp.exp(-jnp.abs(x)))
    cum = _lane_cumsum(logf, 2 * LANES)

    hi, lo, lo2 = _bf16_pieces(cum * LOG2E)

    sub = lax.broadcasted_iota(jnp.int32, (8, seq), 0)
    rows8 = lambda v, r: jnp.broadcast_to(v[r:r + 1, :], (8, seq))
    pad = jnp.zeros((HEAD_DIM - 8, seq), F32)
    q_tiles, k_tiles = [], []
    for r in range(2):
        h8, l8, l28 = rows8(hi, r), rows8(lo, r), rows8(lo2, r)
        q_tiles += [jnp.where(sub == 0, h8, jnp.where(sub == 1, l8, jnp.where(sub == 2, l28,
                              jnp.where(sub < 6, 1.0, 0.0)))), pad]
        k_tiles += [jnp.where(sub < 3, 1.0, jnp.where(sub == 3, -h8, jnp.where(sub == 4, -l8,
                              jnp.where(sub == 5, -l28, 0.0)))), pad]
    aq_t = jnp.concatenate(q_tiles, axis=0).astype(BF16)
    ak = jnp.concatenate(k_tiles, axis=0).T

    lane = lax.broadcasted_iota(jnp.int32, (1, LANES), 1)
    qt = qt_ref[0]
    kv = k_ref[0]
    zeros_t = jnp.zeros((HEAD_DIM, seq), BF16)
    for h in range(2):
        in_head = (lane >= HEAD_DIM * h) & (lane < HEAD_DIM * (h + 1))
        qa_ref[h, :LANES, :] = (jnp.concatenate([qt[:HEAD_DIM], zeros_t], axis=0) if h == 0 else
                                jnp.concatenate([zeros_t, qt[HEAD_DIM:]], axis=0))
        qa_ref[h, LANES:, :] = aq_t
        ka_ref[h, :, :LANES] = kv
        ka_ref[h, :, LANES:] = (ak * in_head.astype(F32)).astype(BF16)

    for h in range(2):
        vx_ref[h, :HEAD_DIM, :] = vt_ref[0, HEAD_DIM * h:HEAD_DIM * (h + 1), :]
        vx_ref[h, HEAD_DIM:, :] = jnp.ones((BF16_SUBLANES, seq), BF16)

    key_le_query = (lax.broadcasted_iota(jnp.int32, (tq, tq), 0)
                    <= lax.broadcasted_iota(jnp.int32, (tq, tq), 1))

    n_q = seq // tq
    assert n_q % 2 == 0, n_q
    steps = []
    for t in range(n_q + 1):
        for a in range(n_q // 2):
            j, iq = (t, a) if t <= a else (t - a - 1, n_q - 1 - a)
            steps += [(j, iq, 0), (j, iq, 1)]
    state = {}
    outs = {}

    def scores(j, iq, h):
        s = jnp.dot(ka_ref[h, j * tq:(j + 1) * tq, :], qa_ref[h, :, iq * tq:(iq + 1) * tq],
                    preferred_element_type=F32)
        return jnp.where(key_le_query, s, NEG) if j == iq else s

    def absorb(j, iq, h, s):
        m = jnp.max(s, axis=0, keepdims=True)
        if j > 0:
            m_old, acc_old = state[iq, h]
            m = jnp.maximum(m_old, m)
        acc = jnp.dot(vx_ref[h, :, j * tq:(j + 1) * tq], jnp.exp2(s - m).astype(BF16),
                      preferred_element_type=F32)
        if j > 0:
            acc = acc_old * jnp.exp2(m_old - m) + acc
        state[iq, h] = (m, acc)
        if j == iq:
            del state[iq, h]
            outs[iq, h] = acc[:HEAD_DIM, :] / acc[HEAD_DIM:HEAD_DIM + 1, :]
            if h == 1:
                o_ref[0, iq * tq:(iq + 1) * tq, :] = jnp.concatenate(
                    [outs.pop((iq, 0)), outs.pop((iq, 1))], axis=0).T

    pending = []
    for step in steps:
        pending.append((step, scores(*step)))
        if len(pending) > FOX_LOOKAHEAD:
            done, s = pending.pop(0)
            absorb(*done, s)
    for done, s in pending:
        absorb(*done, s)


def _fox(k, qt, vt, ft, bf, *, tq):
    B, S, _ = k.shape
    tok = pl.BlockSpec((1, S, LANES), lambda b, p: (b, 0, p))
    feat = lambda rows: pl.BlockSpec((1, rows, S), lambda b, p: (b, p, 0))
    return pl.pallas_call(
        functools.partial(_fox_kernel, seq=S, tq=tq),
        grid=(B, HEAD_PAIRS),
        in_specs=[tok, feat(LANES), feat(LANES), feat(8),
                  pl.BlockSpec((8, S), lambda b, p: (p, 0))],
        out_specs=tok,
        out_shape=jax.ShapeDtypeStruct((B, S, ATT_WIDTH), F32),
        scratch_shapes=[pltpu.VMEM((2, 2 * LANES, S), BF16),
                        pltpu.VMEM((2, S, 2 * LANES), BF16),
                        pltpu.VMEM((2, HEAD_DIM + BF16_SUBLANES, S), BF16)],
        compiler_params=_params("parallel", "parallel"),
        name="fox",
    )(k, qt, vt, ft, bf)


MXU_WIDTH = 256
MIX_GROUPS = 2


def _hidden_chunks(d_ff, target):
    assert d_ff % MXU_WIDTH == 0, d_ff
    tiles = d_ff // MXU_WIDTH
    n = -(-d_ff // target)
    sizes = [(tiles // n + (i < tiles % n)) * MXU_WIDTH for i in range(n)]
    starts = [sum(sizes[:i]) for i in range(n)]
    return [slice(a, a + w) for a, w in zip(starts, sizes)]


def _mixffn_kernel(x_ref, ys_ref, yd_ref, yf_ref, gs_ref, gd_ref, gf_ref, wo_ref,
                   gp_ref, g1_ref, wgu_ref, wdn_ref, g2_ref, o_ref, *, f_target):
    tm = x_ref.shape[1]
    groups = [slice(i * tm // MIX_GROUPS, (i + 1) * tm // MIX_GROUPS) for i in range(MIX_GROUPS)]
    dot = lambda a, b: jnp.dot(a, b, preferred_element_type=F32)

    def mix_in(r):
        mixed = jnp.concatenate([_rms_scale(ys_ref[0, r, :], gs_ref[...]).astype(BF16),
                                 _rms_scale(yd_ref[0, r, :], gd_ref[...]).astype(BF16),
                                 _rms_scale(yf_ref[0, r, :], gf_ref[...]).astype(BF16)], axis=1)
        return dot(mixed, wo_ref[...])

    o = [mix_in(r) for r in groups]
    x, h, acc = [], [], []
    d_ff = wdn_ref.shape[0]
    chunks = _hidden_chunks(d_ff, f_target)
    gate_up = lambda hg, cols: (dot(hg, wgu_ref[:, cols]),
                                dot(hg, wgu_ref[:, d_ff + cols.start:d_ff + cols.stop]))
    gu = []
    for g, r in enumerate(groups):
        x.append(x_ref[0, r, :] + _rms_scale(o[g], gp_ref[...]))
        h.append(_rms_scale(x[g], g1_ref[...]).astype(BF16))
        gu.append(gate_up(h[g], chunks[0]))
    for c, cols in enumerate(chunks):
        nxt = []
        for g in range(MIX_GROUPS):
            gate, up = gu[g]
            act = (gate * jax.nn.sigmoid(gate) * up).astype(BF16)
            if c + 1 < len(chunks):
                nxt.append(gate_up(h[g], chunks[c + 1]))
            down = dot(act, wdn_ref[cols, :])
            if c == 0:
                acc.append(down)
            else:
                acc[g] = acc[g] + down
        gu = nxt
    for g, r in enumerate(groups):
        o_ref[0, r, :] = x[g] + _rms_scale(acc[g], g2_ref[...])


def _mixffn(x, ys_sb, yd, yf, gs, gd, gf, wo, gp, g1, wgu, wdn, g2, *, layer, tm, f_target):
    B, S, D = x.shape
    const = lambda b, j: (0, 0)
    tok = lambda w: pl.BlockSpec((1, tm, w), lambda b, j: (b, j, 0))

    def full(a):
        if a.ndim == 2:
            return pl.BlockSpec(a.shape, const)
        return pl.BlockSpec((None,) + a.shape[1:], lambda b, j: (layer, 0, 0),
                            pipeline_mode=pl.Buffered(1))

    weights = (gs, gd, gf, wo, gp, g1, wgu, wdn, g2)
    return pl.pallas_call(
        functools.partial(_mixffn_kernel, f_target=f_target),
        grid=(B, S // tm),
        in_specs=[tok(D), tok(SSM_WIDTH), tok(ATT_WIDTH),
                  tok(ATT_WIDTH)] + [full(a) for a in weights],
        out_specs=tok(D),
        out_shape=jax.ShapeDtypeStruct((B, S, D), F32),
        compiler_params=_params("parallel", "parallel"),
        name="mixffn",
    )(x, ys_sb, yd, yf, *weights)


def _s5_matrices(log_dt, a_re, a_im, b_re, b_im, c_re, c_im):
    G, P = a_re.shape
    C = b_re.shape[-1]
    dt = jnp.exp(log_dt)[:, None]
    decay = jnp.exp(a_re * dt)
    abar_re = decay * jnp.cos(a_im * dt)
    abar_im = decay * jnp.sin(a_im * dt)
    mag2 = a_re * a_re + a_im * a_im
    coef_re = ((abar_re - 1.0) * a_re + abar_im * a_im) / mag2
    coef_im = (abar_im * a_re - (abar_re - 1.0) * a_im) / mag2
    bbar_re = coef_re[..., None] * b_re - coef_im[..., None] * b_im
    bbar_im = coef_re[..., None] * b_im + coef_im[..., None] * b_re
    eye = jnp.eye(G, dtype=F32)
    bt_re = jnp.einsum('gpc,gh->gchp', bbar_re, eye).reshape(G * C, G * P)
    bt_im = jnp.einsum('gpc,gh->gchp', bbar_im, eye).reshape(G * C, G * P)
    bt = jnp.concatenate([bt_re, bt_im], axis=1).astype(BF16)
    ct_re = jnp.einsum('gcp,gh->gphc', c_re, eye).reshape(G * P, G * C)
    ct_im = jnp.einsum('gcp,gh->gphc', -c_im, eye).reshape(G * P, G * C)
    ct = jnp.concatenate([ct_re, ct_im], axis=0).astype(BF16)
    return bt, abar_re.reshape(1, G * P), abar_im.reshape(1, G * P), ct


def _per_pair_rows(v):
    out = jnp.zeros((HEAD_PAIRS, 8) + v.shape[1:], v.dtype)
    out = out.at[:, :2].set(v.reshape((HEAD_PAIRS, 2) + v.shape[1:]))
    return out.reshape((HEAD_PAIRS * 8,) + v.shape[1:])


def kernel(x, g_pre_mix, w_in, b_f, ssm_log_dt, ssm_a_re, ssm_a_im, ssm_b_re, ssm_b_im, ssm_c_re, ssm_c_im, ssm_d, w_glu, b_glu, g_mix, w_out, g_post_mix, g_pre_ffn, w_gate_up, w_down, g_post_ffn):
    B, S, D = x.shape
    depth = w_in.shape[0]
    o1 = SSM_WIDTH
    o2 = o1 + 3 * ATT_WIDTH
    o3 = o2 + 2 * ATT_WIDTH
    o4 = o3 + ATT_WIDTH
    row = lambda v: v.reshape(1, -1)
    w_in_bf, w_out_bf, w_gate_up_bf, w_down_bf = (
        w.astype(BF16) for w in (w_in, w_out, w_gate_up, w_down))

    for l in range(depth):
        w = w_in_bf[l]
        u_ssm, qkv_dsa, k_fox, qt_fox, vt_fox, ft = _premix(
            x, row(g_pre_mix[l]), w[:, :o1],
            jnp.concatenate([w[:, o1:o2], w[:, o2 + ATT_WIDTH:o3]], axis=1),
            w[:, o2:o2 + ATT_WIDTH].T, w[:, o3:o4].T, _per_pair_rows(w[:, o4:].T),
            tm=PREMIX_TILE)

        bt, a_re, a_im, ct = _s5_matrices(ssm_log_dt[l], ssm_a_re[l], ssm_a_im[l], ssm_b_re[l],
                                          ssm_b_im[l], ssm_c_re[l], ssm_c_im[l])
        y_ssm = _s5(u_ssm, bt, a_re, a_im, ct, row(ssm_d[l]), w_glu[l].astype(BF16),
                    row(b_glu[l]), steps=S5_STEPS)

        y_dsa = _dsa(qkv_dsa)

        bf = jnp.broadcast_to(_per_pair_rows(b_f[l][:, None]), (8 * HEAD_PAIRS, S))
        y_fox = _fox(k_fox, qt_fox, vt_fox, ft, bf, tq=FOX_BLOCK)

        gm = g_mix[l]
        x = _mixffn(x, y_ssm, y_dsa, y_fox,
                    row(gm[:o1]), row(gm[o1:o1 + ATT_WIDTH]), row(gm[o1 + ATT_WIDTH:]),
                    w_out_bf, row(g_post_mix[l]), row(g_pre_ffn[l]), w_gate_up_bf, w_down_bf,
                    row(g_post_ffn[l]), layer=l, tm=MIX_TILE, f_target=FFN_CHUNK)
    return x
```
